```python
import math
import jax
import jax.numpy as jnp
from jax import lax
import numpy as np

D_MODEL = 1024
BATCH = 4
SEQ = 4096
DEPTH = 2

GRID_W = 64
CTX_LEN = 256
HEAD_DIM = 64
ROPE_BASE = 10000.0
EPS = 1e-6
BLK = 128
A_HEADS = 8
A_KV_HEADS = 2
WINDOW = 128
SSM_HEADS = 16
SSM_HEADDIM = 64
SSM_INNER = SSM_HEADS * SSM_HEADDIM
SSM_GROUPS = 2
SSM_STATE = 128
SSM_CONV = 3
SSM_CHUNK = 128
SSM_BC = SSM_GROUPS * SSM_STATE
SSM_XBC = SSM_INNER + 2 * SSM_BC
C_HEADS = 8
C_KV_HEADS = 2
D_FF = 2816
FFN_CONV = 3
A_Q_W = A_HEADS * HEAD_DIM
A_KV_W = A_KV_HEADS * HEAD_DIM
C_Q_W = C_HEADS * HEAD_DIM
C_KV_W = C_KV_HEADS * HEAD_DIM
IN_WIDTH = (A_Q_W + 2 * A_KV_W) + (SSM_INNER + SSM_XBC + 2 * SSM_HEADS) + (C_Q_W + 2 * C_KV_W) + 3 * D_MODEL

kernel_name = 'hybrid_swa_ssd_gridattn_convffn_prefix'


def _in_spans():
    sizes = (('a_q', A_Q_W), ('a_k', A_KV_W), ('a_v', A_KV_W),
             ('b_z', SSM_INNER), ('b_xbc', SSM_XBC), ('b_dt', 2 * SSM_HEADS),
             ('c_q', C_Q_W), ('c_k', C_KV_W), ('c_v', C_KV_W),
             ('gates', 3 * D_MODEL))
    spans, start = {}, 0
    for name, n in sizes:
        spans[name] = (start, start + n)
        start += n
    return spans


def rms_norm(x, g):
    xf = x.astype(jnp.float32)
    y = xf * lax.rsqrt(jnp.mean(xf * xf, axis=-1, keepdims=True) + EPS)
    return (y * g.astype(jnp.float32)).astype(x.dtype)


def modulate(h, shift, scale):
    return h * (1 + scale) + shift


def grid_rope(rows):
    t_row = jnp.repeat(jnp.arange(rows), GRID_W).astype(jnp.float32)
    t_col = jnp.tile(jnp.arange(GRID_W), rows).astype(jnp.float32)
    n = HEAD_DIM // 4
    inv = ROPE_BASE ** (-jnp.arange(n, dtype=jnp.float32) / n)
    ang = jnp.concatenate([t_row[:, None] * inv, t_col[:, None] * inv], axis=-1)
    return jnp.cos(ang), jnp.sin(ang)


def apply_rope(x, cos, sin):
    b, L, h, dh = x.shape
    xr = x.astype(jnp.float32).reshape(b, L, h, dh // 2, 2)
    c = cos[None, :, None, :]
    s = sin[None, :, None, :]
    x1, x2 = xr[..., 0], xr[..., 1]
    out = jnp.stack([x1 * c - x2 * s, x1 * s + x2 * c], axis=-1)
    return out.reshape(b, L, h, dh).astype(x.dtype)


def dwconv(u, w, bias):
    k = w.shape[0]
    pad = k // 2
    y = lax.conv_general_dilated(u, w[:, None, :].astype(u.dtype), window_strides=(1,),
                                 padding=[(pad, pad)], dimension_numbers=('NWC', 'WIO', 'NWC'),
                                 feature_group_count=u.shape[-1])
    return y + bias.astype(u.dtype)


def dense_attention(q, k, v, sink):
    b, lq, hq, dh = q.shape
    hkv = k.shape[2]
    g = hq // hkv
    qg = q.reshape(b, lq, hkv, g, dh)
    s = jnp.einsum('bqhgd,bkhd->bhgqk', qg, k).astype(jnp.float32) * (dh ** -0.5)
    if sink is not None:
        s_sink = jnp.broadcast_to(sink.astype(jnp.float32).reshape(1, hkv, g, 1, 1), s.shape[:-1] + (1,))
        s = jnp.concatenate([s, s_sink], axis=-1)
    p = jax.nn.softmax(s, axis=-1)
    if sink is not None:
        p = p[..., :-1]
    o = jnp.einsum('bhgqk,bkhd->bqhgd', p.astype(v.dtype), v)
    return o.reshape(b, lq, hq * dh)


def window_attention(q, k, v, kc, vc, sink):
    b, L, hq, dh = q.shape
    hkv = k.shape[2]
    g = hq // hkv
    nb = L // BLK
    lc = kc.shape[1]
    qb = q.reshape(b, nb, BLK, hkv, g, dh)
    pad = ((0, 0), (BLK, BLK), (0, 0), (0, 0))
    kp = jnp.pad(k, pad).reshape(b, nb + 2, BLK, hkv, dh)
    vp = jnp.pad(v, pad).reshape(b, nb + 2, BLK, hkv, dh)
    kw = jnp.concatenate([kp[:, :-2], kp[:, 1:-1], kp[:, 2:]], axis=2)
    vw = jnp.concatenate([vp[:, :-2], vp[:, 1:-1], vp[:, 2:]], axis=2)
    scale = dh ** -0.5
    s_loc = jnp.einsum('bnqhgd,bnjhd->bnhgqj', qb, kw).astype(jnp.float32) * scale
    s_ctx = jnp.einsum('bnqhgd,bchd->bnhgqc', qb, kc).astype(jnp.float32) * scale
    q_pos = (jnp.arange(nb) * BLK)[:, None] + jnp.arange(BLK)[None, :]
    k_pos = (jnp.arange(nb) * BLK - BLK)[:, None] + jnp.arange(3 * BLK)[None, :]
    rel = q_pos[:, :, None] - k_pos[:, None, :]
    valid = (jnp.abs(rel) <= WINDOW) & (k_pos[:, None, :] >= 0) & (k_pos[:, None, :] < L)
    s_loc = jnp.where(valid[None, :, None, None], s_loc, -jnp.inf)
    s_sink = jnp.broadcast_to(sink.astype(jnp.float32).reshape(1, 1, hkv, g, 1, 1), s_loc.shape[:-1] + (1,))
    p = jax.nn.softmax(jnp.concatenate([s_loc, s_ctx, s_sink], axis=-1), axis=-1)
    nloc = 3 * BLK
    p_loc = p[..., :nloc].astype(v.dtype)
    p_ctx = p[..., nloc:nloc + lc].astype(v.dtype)
    o = jnp.einsum('bnhgqj,bnjhd->bnqhgd', p_loc, vw) + jnp.einsum('bnhgqc,bchd->bnqhgd', p_ctx, vc)
    return o.reshape(b, L, hq * dh)


def grid_attention(q, k, v, kc, vc):
    b, L, hq, dh = q.shape
    nb = L // BLK
    k_all = jnp.concatenate([k, kc], axis=1)
    v_all = jnp.concatenate([v, vc], axis=1)
    qb = jnp.moveaxis(q.reshape(b, nb, BLK, hq, dh), 1, 0)
    ob = lax.map(lambda qi: dense_attention(qi, k_all, v_all, None), qb)
    return jnp.moveaxis(ob, 0, 1).reshape(b, L, hq * dh)


def ssm_inputs(xbc_raw, dt_raw, conv_w, conv_b, dt_bias):
    b, L, _ = xbc_raw.shape
    xbc = jax.nn.silu(dwconv(xbc_raw, conv_w, conv_b))
    xs = xbc[..., :SSM_INNER].reshape(b, L, SSM_HEADS, SSM_HEADDIM)
    bm = xbc[..., SSM_INNER:SSM_INNER + SSM_BC].reshape(b, L, SSM_GROUPS, SSM_STATE)
    cm = xbc[..., SSM_INNER + SSM_BC:].reshape(b, L, SSM_GROUPS, SSM_STATE)
    dt = jax.nn.softplus(dt_raw.astype(jnp.float32).reshape(b, L, 2, SSM_HEADS) + dt_bias.astype(jnp.float32))
    return xs, bm, cm, dt


def ssd_scan(xh, dt, a_coef, bm, cm, h0, want_y):
    b, L, H, P = xh.shape
    G, N = bm.shape[2], bm.shape[3]
    hg = H // G
    q = SSM_CHUNK
    nc = L // q
    f32 = jnp.float32
    xdt = (xh.astype(f32) * dt[..., None]).reshape(b, nc, q, G, hg, P)
    bc = bm.astype(f32).reshape(b, nc, q, G, N)
    cc = cm.astype(f32).reshape(b, nc, q, G, N)
    acs = jnp.cumsum((dt * a_coef.astype(f32)).reshape(b, nc, q, G, hg), axis=2)
    decay_end = jnp.exp(acs[:, :, -1:] - acs)
    states = jnp.einsum('bcjgn,bcjghp->bcghpn', bc, xdt * decay_end[..., None])
    chunk_decay = jnp.exp(acs[:, :, -1])

    def step(h, inp):
        st, dec = inp
        return h * dec[..., None, None] + st, h

    h_last, h_in = lax.scan(step, h0.reshape(b, G, hg, P, N),
                            (jnp.moveaxis(states, 1, 0), jnp.moveaxis(chunk_decay, 1, 0)))
    h_last = h_last.reshape(b, H, P, N)
    if not want_y:
        return None, h_last
    h_in = jnp.moveaxis(h_in, 0, 1)
    seg = acs[:, :, :, None] - acs[:, :, None, :]
    tri = jnp.tril(jnp.ones((q, q), dtype=bool))
    lmat = jnp.exp(jnp.where(tri[:, :, None, None], seg, -jnp.inf))
    cb = jnp.einsum('bcign,bcjgn->bcijg', cc, bc)
    y_diag = jnp.einsum('bcijgh,bcjghp->bcighp', cb[..., None] * lmat, xdt)
    y_off = jnp.einsum('bcign,bcghpn->bcighp', cc, h_in) * jnp.exp(acs)[..., None]
    y = (y_diag + y_off).reshape(b, L, H, P).astype(xh.dtype)
    return y, h_last


def ssm_output(yf, yb, xs, z, d_skip, norm_g):
    b, L = xs.shape[0], xs.shape[1]
    y = yf + yb + xs * d_skip[:, None].astype(xs.dtype)
    y = y.reshape(b, L, SSM_INNER)
    return rms_norm(y * jax.nn.silu(z), norm_g)


def merge_branches(ya, yb, yc, g_raw, w_oa, w_ob, w_oc, w_out):
    g = jax.nn.sigmoid(g_raw.astype(jnp.float32)).astype(ya.dtype)
    ga, gb, gc = jnp.split(g, 3, axis=-1)
    m = ga * (ya @ w_oa) + gb * (yb @ w_ob) + gc * (yc @ w_oc)
    return m @ w_out


def conv_ffn(h, w_up, w_gate, conv_w, conv_b, w_down):
    up = h @ w_up
    gt = dwconv(h @ w_gate, conv_w, conv_b)
    return (jax.nn.silu(gt) * up) @ w_down


def hybrid_layer(x, xc, c_mod, cc_mod, cos, sin, w_in, norm1, norm2, a_sink, ssm_conv_w, ssm_conv_b,
                 ssm_a_log, ssm_dt_bias, ssm_d, ssm_norm, c_q_norm, c_k_norm, w_oa, w_ob, w_oc, w_out,
                 ffn_w_up, ffn_w_gate, ffn_conv_w, ffn_conv_b, ffn_w_down, need_ctx_out):
    b, L, _ = x.shape
    lc = xc.shape[1]
    sp = _in_spans()
    shift1, scale1, gate1, shift2, scale2, gate2 = jnp.split(c_mod[:, None, :], 6, axis=-1)
    h = modulate(rms_norm(x, norm1), shift1, scale1)
    hc = modulate(rms_norm(xc, norm1), cc_mod[:D_MODEL], cc_mod[D_MODEL:2 * D_MODEL])
    u = h @ w_in

    def lat(name):
        return u[..., sp[name][0]:sp[name][1]]

    def ctxp(name):
        return hc @ w_in[:, sp[name][0]:sp[name][1]]

    qa = apply_rope(lat('a_q').reshape(b, L, A_HEADS, HEAD_DIM), cos, sin)
    ka = apply_rope(lat('a_k').reshape(b, L, A_KV_HEADS, HEAD_DIM), cos, sin)
    va = lat('a_v').reshape(b, L, A_KV_HEADS, HEAD_DIM)
    kac = ctxp('a_k').reshape(b, lc, A_KV_HEADS, HEAD_DIM)
    vac = ctxp('a_v').reshape(b, lc, A_KV_HEADS, HEAD_DIM)
    ya = window_attention(qa, ka, va, kac, vac, a_sink)

    qg = apply_rope(rms_norm(lat('c_q').reshape(b, L, C_HEADS, HEAD_DIM), c_q_norm), cos, sin)
    kg = apply_rope(rms_norm(lat('c_k').reshape(b, L, C_KV_HEADS, HEAD_DIM), c_k_norm), cos, sin)
    vg = lat('c_v').reshape(b, L, C_KV_HEADS, HEAD_DIM)
    kgc = rms_norm(ctxp('c_k').reshape(b, lc, C_KV_HEADS, HEAD_DIM), c_k_norm)
    vgc = ctxp('c_v').reshape(b, lc, C_KV_HEADS, HEAD_DIM)
    yg = grid_attention(qg, kg, vg, kgc, vgc)

    a_coef = -jnp.exp(ssm_a_log.astype(jnp.float32))
    xs_c, bm_c, cm_c, dt_c = ssm_inputs(ctxp('b_xbc'), ctxp('b_dt'), ssm_conv_w, ssm_conv_b, ssm_dt_bias)
    h0 = jnp.zeros((b, SSM_HEADS, SSM_HEADDIM, SSM_STATE), jnp.float32)
    yf_c, hf_c = ssd_scan(xs_c, dt_c[:, :, 0], a_coef[0], bm_c, cm_c, h0, need_ctx_out)
    yb_c, hb_c = ssd_scan(jnp.flip(xs_c, 1), jnp.flip(dt_c[:, :, 1], 1), a_coef[1],
                          jnp.flip(bm_c, 1), jnp.flip(cm_c, 1), h0, need_ctx_out)
    xs, bm, cm, dt = ssm_inputs(lat('b_xbc'), lat('b_dt'), ssm_conv_w, ssm_conv_b, ssm_dt_bias)
    yf, _ = ssd_scan(xs, dt[:, :, 0], a_coef[0], bm, cm, hf_c, True)
    yb, _ = ssd_scan(jnp.flip(xs, 1), jnp.flip(dt[:, :, 1], 1), a_coef[1],
                     jnp.flip(bm, 1), jnp.flip(cm, 1), hb_c, True)
    ys = ssm_output(yf, jnp.flip(yb, 1), xs, lat('b_z'), ssm_d, ssm_norm)

    x = x + gate1 * merge_branches(ya, ys, yg, lat('gates'), w_oa, w_ob, w_oc, w_out)
    h2 = modulate(rms_norm(x, norm2), shift2, scale2)
    x = x + gate2 * conv_ffn(h2, ffn_w_up, ffn_w_gate, ffn_conv_w, ffn_conv_b, ffn_w_down)
    if not need_ctx_out:
        return x, None

    cgate1 = cc_mod[2 * D_MODEL:3 * D_MODEL]
    cshift2 = cc_mod[3 * D_MODEL:4 * D_MODEL]
    cscale2 = cc_mod[4 * D_MODEL:5 * D_MODEL]
    cgate2 = cc_mod[5 * D_MODEL:]
    yac = dense_attention(ctxp('a_q').reshape(b, lc, A_HEADS, HEAD_DIM), kac, vac, a_sink)
    ygc = dense_attention(rms_norm(ctxp('c_q').reshape(b, lc, C_HEADS, HEAD_DIM), c_q_norm), kgc, vgc, None)
    ysc = ssm_output(yf_c, jnp.flip(yb_c, 1), xs_c, ctxp('b_z'), ssm_d, ssm_norm)
    xc = xc + cgate1 * merge_branches(yac, ysc, ygc, ctxp('gates'), w_oa, w_ob, w_oc, w_out)
    h2c = modulate(rms_norm(xc, norm2), cshift2, cscale2)
    xc = xc + cgate2 * conv_ffn(h2c, ffn_w_up, ffn_w_gate, ffn_conv_w, ffn_conv_b, ffn_w_down)
    return x, xc


def setup_inputs(seed: int = 0) -> dict:
    key = jax.random.key(seed)
    ks = jax.random.split(key, 32)
    f32 = jnp.float32

    def nrm(k, shape, scale):
        return jax.random.normal(k, shape, f32) * scale

    d = D_MODEL
    dt0 = jnp.exp(jax.random.uniform(ks[11], (DEPTH, 2, SSM_HEADS), f32, math.log(1e-3), math.log(1e-1)))
    return {
        'x': nrm(ks[0], (BATCH, SEQ, d), 1.0),
        'c': nrm(ks[1], (BATCH, d), 1.0),
        'ctx': nrm(ks[2], (BATCH, CTX_LEN, d), 1.0),
        'c_ctx': nrm(ks[3], (d,), 1.0),
        'w_mod': nrm(ks[4], (DEPTH, d, 6 * d), 0.5 * d ** -0.5),
        'b_mod': nrm(ks[5], (DEPTH, 6 * d), 0.02),
        'norm1': 1.0 + nrm(ks[6], (DEPTH, d), 0.05),
        'norm2': 1.0 + nrm(ks[7], (DEPTH, d), 0.05),
        'w_in': nrm(ks[8], (DEPTH, d, IN_WIDTH), d ** -0.5),
        'a_sink': nrm(ks[9], (DEPTH, A_HEADS), 0.5),
        'ssm_conv_w': nrm(ks[10], (DEPTH, SSM_CONV, SSM_XBC), SSM_CONV ** -0.5),
        'ssm_conv_b': nrm(ks[12], (DEPTH, SSM_XBC), 0.02),
        'ssm_A_log': jnp.log(jax.random.uniform(ks[13], (DEPTH, 2, SSM_HEADS), f32, 1.0, 16.0)),
        'ssm_dt_bias': dt0 + jnp.log(-jnp.expm1(-dt0)),
        'ssm_D': 1.0 + nrm(ks[14], (DEPTH, SSM_HEADS), 0.1),
        'ssm_norm': 1.0 + nrm(ks[15], (DEPTH, SSM_INNER), 0.05),
        'c_q_norm': 1.0 + nrm(ks[16], (DEPTH, HEAD_DIM), 0.05),
        'c_k_norm': 1.0 + nrm(ks[17], (DEPTH, HEAD_DIM), 0.05),
        'w_oa': nrm(ks[18], (DEPTH, A_Q_W, d), A_Q_W ** -0.5),
        'w_ob': nrm(ks[19], (DEPTH, SSM_INNER, d), SSM_INNER ** -0.5),
        'w_oc': nrm(ks[20], (DEPTH, C_Q_W, d), C_Q_W ** -0.5),
        'w_out': nrm(ks[21], (DEPTH, d, d), d ** -0.5),
        'ffn_w_up': nrm(ks[22], (DEPTH, d, D_FF), d ** -0.5),
        'ffn_w_gate': nrm(ks[23], (DEPTH, d, D_FF), d ** -0.5),
        'ffn_conv_w': nrm(ks[24], (DEPTH, FFN_CONV, D_FF), FFN_CONV ** -0.5),
        'ffn_conv_b': nrm(ks[25], (DEPTH, D_FF), 0.02),
        'ffn_w_down': nrm(ks[26], (DEPTH, D_FF, d), D_FF ** -0.5),
        'final_norm': 1.0 + nrm(ks[27], (d,), 0.05),
    }


def reference(x, c, ctx, c_ctx, w_mod, b_mod, norm1, norm2, w_in, a_sink, ssm_conv_w, ssm_conv_b,
              ssm_A_log, ssm_dt_bias, ssm_D, ssm_norm, c_q_norm, c_k_norm, w_oa, w_ob, w_oc, w_out,
              ffn_w_up, ffn_w_gate, ffn_conv_w, ffn_conv_b, ffn_w_down, final_norm):
    rows = x.shape[1] // GRID_W
    cos, sin = grid_rope(rows)
    xc = ctx
    sc = jax.nn.silu(c)
    scc = jax.nn.silu(c_ctx)
    for l in range(DEPTH):
        need_ctx_out = l < DEPTH - 1
        n_cmod = 6 * D_MODEL if need_ctx_out else 2 * D_MODEL
        c_mod = sc @ w_mod[l] + b_mod[l]
        cc_mod = scc @ w_mod[l][:, :n_cmod] + b_mod[l][:n_cmod]
        x, xc = hybrid_layer(x, xc, c_mod, cc_mod, cos, sin, w_in[l], norm1[l], norm2[l], a_sink[l],
                             ssm_conv_w[l], ssm_conv_b[l], ssm_A_log[l], ssm_dt_bias[l], ssm_D[l],
                             ssm_norm[l], c_q_norm[l], c_k_norm[l], w_oa[l], w_ob[l], w_oc[l], w_out[l],
                             ffn_w_up[l], ffn_w_gate[l], ffn_conv_w[l], ffn_conv_b[l], ffn_w_down[l],
                             need_ctx_out)
    return rms_norm(x, final_norm)
```

```python
import functools
import math

import jax
import jax.numpy as jnp
from jax import lax
from jax.experimental import pallas as pl
from jax.experimental.pallas import tpu as pltpu

F32 = jnp.float32
BF16 = jnp.bfloat16

D_MODEL = 1024
GRID_W = 64
HEAD_DIM = 64
ROPE_BASE = 10000.0
EPS = 1e-6
WINDOW = 128
N_HEADS = 8
N_KV = 2
QKV_W = (N_HEADS + 2 * N_KV) * HEAD_DIM
SSM_HEADS = 16
SSM_P = 64
SSM_INNER = SSM_HEADS * SSM_P
SSM_GROUPS = 2
SSM_N = 128
SSM_CHUNK = 128
SSM_XBC = SSM_INNER + 2 * SSM_GROUPS * SSM_N
D_FF = 2816

LANES = 128
SUBLANES = 8
VMEM_LIMIT_BYTES = 56 * 1024 * 1024

NEG_INF = float("-inf")


def _params(*sem):
    return pltpu.CompilerParams(dimension_semantics=sem, vmem_limit_bytes=VMEM_LIMIT_BYTES)


def _full(shape):
    n = len(shape)
    return pl.BlockSpec(shape, lambda *_: (0,) * n)


def _sigmoid(x):
    return 1.0 / (1.0 + jnp.exp(-x))


def _silu(x):
    return x * _sigmoid(x)


def _norm_mod(x, g, shift, scale):
    ms = jnp.mean(x * x, axis=-1, keepdims=True)
    return (x * lax.rsqrt(ms + EPS) * g) * (1.0 + scale) + shift


def _dot(a, b):
    return jnp.dot(a, b, preferred_element_type=F32)


def _dot_nt(a, b):
    return lax.dot_general(a, b, (((1,), (1,)), ((), ())), preferred_element_type=F32)


def _split3(x):
    hi = x.astype(BF16)
    r1 = x - hi.astype(F32)
    mid = r1.astype(BF16)
    lo = (r1 - mid.astype(F32)).astype(BF16)
    return hi, mid, lo


def _mod_kernel(c_ref, w_ref, b_ref, o_ref):
    s = _silu(c_ref[...])
    o_ref[...] = jnp.dot(s, w_ref[...], preferred_element_type=F32,
                         precision=lax.Precision.HIGHEST) + b_ref[...]


def _mod_call(c8, w, b):
    d, n = w.shape
    bn = 1536
    return pl.pallas_call(
        _mod_kernel,
        grid=(n // bn,),
        in_specs=[_full((SUBLANES, d)),
                  pl.BlockSpec((d, bn), lambda j: (0, j)),
                  pl.BlockSpec((1, bn), lambda j: (0, j))],
        out_specs=pl.BlockSpec((SUBLANES, bn), lambda j: (0, j)),
        out_shape=jax.ShapeDtypeStruct((SUBLANES, n), F32),
        compiler_params=_params("arbitrary"),
        name="mod_proj",
    )(c8, w, b.reshape(1, n))


def _mod_spec(bm, seq_len, per_seq):
    if per_seq:
        return pl.BlockSpec((None, 6, D_MODEL), lambda i: ((i * bm) // seq_len, 0, 0))
    return pl.BlockSpec((None, 6, D_MODEL), lambda i: (0, 0, 0))


def _rope(y, cos, sin_signed):
    lane = lax.broadcasted_iota(jnp.int32, y.shape, 1)
    swapped = jnp.where((lane & 1) == 0, pltpu.roll(y, LANES - 1, 1), pltpu.roll(y, 1, 1))
    return y * cos + swapped * sin_signed


def _inproj_attn_kernel(*refs, rope, qknorm):
    x_ref, mod_ref, g_ref, w_ref = refs[:4]
    k = 4
    if rope:
        cos_ref, sin_ref = refs[k:k + 2]
        k += 2
    if qknorm:
        qg_ref, kg_ref, pool_ref = refs[k:k + 3]
        k += 3
    o_ref = refs[k]
    h = _norm_mod(x_ref[...], g_ref[...], mod_ref[0:1, :], mod_ref[1:2, :]).astype(BF16)
    u = _dot(h, w_ref[...])
    n_q = N_HEADS * HEAD_DIM // LANES
    n_qk = n_q + N_KV * HEAD_DIM // LANES
    for t in range(QKV_W // LANES):
        y = u[:, t * LANES:(t + 1) * LANES]
        if t < n_qk:
            if qknorm:
                sq = y * y
                hi = sq.astype(BF16)
                lo = (sq - hi.astype(F32)).astype(BF16)
                ss = _dot(hi, pool_ref[...]) + _dot(lo, pool_ref[...])
                gain = qg_ref[...] if t < n_q else kg_ref[...]
                y = y * lax.rsqrt(ss * (1.0 / HEAD_DIM) + EPS) * gain
            if rope:
                y = _rope(y, cos_ref[...], sin_ref[...])
            if t < n_q:
                y = y * (HEAD_DIM ** -0.5)
        o_ref[:, t * LANES:(t + 1) * LANES] = y.astype(BF16)


def _inproj_attn_call(x2d, mod, g, w, bm, seq_len, per_seq, rope_tabs=None, qk_gains=None):
    t, d = x2d.shape
    rope = rope_tabs is not None
    qknorm = qk_gains is not None
    args = [x2d, mod, g.reshape(1, d), w]
    specs = [pl.BlockSpec((bm, d), lambda i: (i, 0)), _mod_spec(bm, seq_len, per_seq),
             _full((1, d)), _full((d, QKV_W))]
    if rope:
        nblk = seq_len // bm
        args += list(rope_tabs)
        specs += [pl.BlockSpec((bm, LANES), lambda i: (i % nblk, 0))] * 2
    if qknorm:
        qg, kg = qk_gains
        lane_head = jnp.arange(LANES) // HEAD_DIM
        pool = (lane_head[:, None] == lane_head[None, :]).astype(BF16)
        args += [jnp.tile(qg, LANES // HEAD_DIM).reshape(1, LANES),
                 jnp.tile(kg, LANES // HEAD_DIM).reshape(1, LANES), pool]
        specs += [_full((1, LANES)), _full((1, LANES)), _full((LANES, LANES))]
    return pl.pallas_call(
        functools.partial(_inproj_attn_kernel, rope=rope, qknorm=qknorm),
        grid=(t // bm,),
        in_specs=specs,
        out_specs=pl.BlockSpec((bm, QKV_W), lambda i: (i, 0)),
        out_shape=jax.ShapeDtypeStruct((t, QKV_W), BF16),
        compiler_params=_params("parallel"),
        name="inproj_attn",
    )(*args)


def _inproj_ssm_kernel(x_ref, mod_ref, g_ref, wz_ref, wx_ref, wdt_ref, z_ref, xbc_ref, dt_ref):
    h = _norm_mod(x_ref[...], g_ref[...], mod_ref[0:1, :], mod_ref[1:2, :]).astype(BF16)
    z_ref[...] = _dot(h, wz_ref[...])
    xbc_ref[...] = _dot(h, wx_ref[...])
    dt_ref[...] = _dot(h, wdt_ref[...])


def _inproj_ssm_call(x2d, mod, g, wz, wx, wdt, bm, seq_len, per_seq):
    t, d = x2d.shape
    row = lambda n: pl.BlockSpec((bm, n), lambda i: (i, 0))
    return pl.pallas_call(
        _inproj_ssm_kernel,
        grid=(t // bm,),
        in_specs=[row(d), _mod_spec(bm, seq_len, per_seq), _full((1, d)),
                  _full((d, SSM_INNER)), _full((d, SSM_XBC)), _full((d, LANES))],
        out_specs=[row(SSM_INNER), row(SSM_XBC), row(LANES)],
        out_shape=[jax.ShapeDtypeStruct((t, SSM_INNER), F32),
                   jax.ShapeDtypeStruct((t, SSM_XBC), F32),
                   jax.ShapeDtypeStruct((t, LANES), F32)],
        compiler_params=_params("parallel"),
        name="inproj_ssm",
    )(x2d, mod, g.reshape(1, d), wz, wx, wdt)


def _attn_kernel(*refs, bq, n_dense, win_len, has_sink):
    q_ref = refs[0]
    k = 1
    dense = []
    for _ in range(n_dense):
        dense.append((refs[k], refs[k + 1]))
        k += 2
    if win_len:
        kl_ref, vl_ref = refs[k:k + 2]
        k += 2
    if has_sink:
        sink_ref = refs[k]
        k += 1
    o_ref = refs[k]

    i = pl.program_id(1)
    p = pl.program_id(2)
    g = p // (N_HEADS // N_KV // 2)
    qf = q_ref[...].astype(F32)
    qr = pltpu.roll(qf, HEAD_DIM, 1)
    lane = lax.broadcasted_iota(jnp.int32, qf.shape, 1)
    in_g = (lane // HEAD_DIM) == g
    q_heads = (jnp.where(in_g, jnp.where(g == 0, qf, qr), 0.0).astype(BF16),
               jnp.where(in_g, jnp.where(g == 1, qf, qr), 0.0).astype(BF16))

    if win_len:
        w = bq + 2 * WINDOW
        start = jnp.clip(i * bq - WINDOW, 0, win_len - w)
        start = pl.multiple_of(start, LANES)
        k_loc = kl_ref[pl.ds(start, w), :]
        v_loc = vl_ref[pl.ds(start, w), :]
        q_pos = i * bq + lax.broadcasted_iota(jnp.int32, (bq, w), 0)
        k_pos = start + lax.broadcasted_iota(jnp.int32, (bq, w), 1)
        valid = jnp.abs(q_pos - k_pos) <= WINDOW

    outs = []
    for hh, qh in enumerate(q_heads):
        scores = []
        values = []
        for k_ref, v_ref in dense:
            scores.append(_dot_nt(qh, k_ref[...]))
            values.append(v_ref[...])
        if win_len:
            scores.append(jnp.where(valid, _dot_nt(qh, k_loc), NEG_INF))
            values.append(v_loc)
        m = scores[0].max(axis=-1, keepdims=True)
        for s in scores[1:]:
            m = jnp.maximum(m, s.max(axis=-1, keepdims=True))
        if has_sink:
            sink = sink_ref[0, 2 * p + hh]
            m = jnp.maximum(m, sink)
            l = jnp.exp(sink - m)
        else:
            l = jnp.zeros_like(m)
        acc = jnp.zeros((bq, LANES), F32)
        for s, v in zip(scores, values):
            e = jnp.exp(s - m)
            l = l + e.sum(axis=-1, keepdims=True)
            acc = acc + _dot(e.astype(BF16), v)
        outs.append(acc / l)
    o_a = jnp.where(g == 0, outs[0], pltpu.roll(outs[0], HEAD_DIM, 1))
    o_b = jnp.where(g == 1, outs[1], pltpu.roll(outs[1], HEAD_DIM, 1))
    o_ref[...] = jnp.where(lane < HEAD_DIM, o_a, o_b).astype(o_ref.dtype)


def _attn_call(q_arr, dense_arrs, win_arr, sink, bq):
    b, lq, _ = q_arr.shape
    k_blk = N_HEADS * HEAD_DIM // LANES
    args = [q_arr]
    specs = [pl.BlockSpec((None, bq, LANES), lambda bi, i, p: (bi, i, p))]
    for arr in dense_arrs:
        lk = arr.shape[1]
        args += [arr, arr]
        specs += [pl.BlockSpec((None, lk, LANES), lambda bi, i, p: (bi, 0, k_blk)),
                  pl.BlockSpec((None, lk, LANES), lambda bi, i, p: (bi, 0, k_blk + 1))]
    win_len = 0
    if win_arr is not None:
        win_len = win_arr.shape[1]
        assert win_len >= bq + 2 * WINDOW
        args += [win_arr, win_arr]
        specs += [pl.BlockSpec((None, win_len, LANES), lambda bi, i, p: (bi, 0, k_blk)),
                  pl.BlockSpec((None, win_len, LANES), lambda bi, i, p: (bi, 0, k_blk + 1))]
    if sink is not None:
        args.append(sink.reshape(1, N_HEADS))
        specs.append(pl.BlockSpec(memory_space=pltpu.SMEM))
    return pl.pallas_call(
        functools.partial(_attn_kernel, bq=bq, n_dense=len(dense_arrs), win_len=win_len,
                          has_sink=sink is not None),
        grid=(b, lq // bq, N_HEADS // 2),
        in_specs=specs,
        out_specs=pl.BlockSpec((None, bq, LANES), lambda bi, i, p: (bi, i, p)),
        out_shape=jax.ShapeDtypeStruct((b, lq, N_HEADS * HEAD_DIM), BF16),
        compiler_params=_params("parallel", "parallel", "arbitrary"),
        name="attn",
    )(*args)


def _pair_bcast(arr, c0, rows):
    lane = lax.broadcasted_iota(jnp.int32, (rows, LANES), 1)
    return jnp.where(lane < SSM_P, arr[:, c0:c0 + 1], arr[:, c0 + 1:c0 + 2])


def _ssd_kernel(*refs, rev, final, nc):
    (x_ref, xp_ref, xn_ref, dt_ref, cw_ref, cb_ref, dtb_ref, alog_ref, h0_ref) = refs[:9]
    k = 9
    if final:
        yo_ref, z_ref, dsk_ref, ng_ref = refs[k:k + 4]
        k += 4
    y_ref, hl_ref, h_ref = refs[k:k + 3]
    if final:
        gated_ref = refs[k + 3]

    c = pl.program_id(1)
    cidx = (nc - 1 - c) if rev else c
    q = SSM_CHUNK

    @pl.when(c == 0)
    def _():
        h_ref[...] = h0_ref[...]

    x = x_ref[...]
    prev = jnp.where(cidx > 0, xp_ref[SUBLANES - 1:SUBLANES, :], 0.0)
    nxt = jnp.where(cidx < nc - 1, xn_ref[0:1, :], 0.0)
    row = lax.broadcasted_iota(jnp.int32, x.shape, 0)
    xm1 = jnp.where(row == 0, prev, pltpu.roll(x, 1, 0))
    xp1 = jnp.where(row == q - 1, nxt, pltpu.roll(x, q - 1, 0))
    xbc = _silu(xm1 * cw_ref[0:1, :] + x * cw_ref[1:2, :] + xp1 * cw_ref[2:3, :] + cb_ref[...])

    dtr = dt_ref[...] + dtb_ref[...]
    dtv = jnp.maximum(dtr, 0.0) + jnp.log(1.0 + jnp.exp(-jnp.abs(dtr)))
    dta = dtv * (-jnp.exp(alog_ref[...]))
    ri = lax.broadcasted_iota(jnp.int32, (q, q), 0)
    ci = lax.broadcasted_iota(jnp.int32, (q, q), 1)
    causal = (ci >= ri) if rev else (ci <= ri)
    tri = causal.astype(BF16)
    tri_t = ((ri >= ci) if rev else (ri <= ci)).astype(BF16)
    dta_t = dta.T
    acs = jnp.zeros((q, LANES), F32)
    acs_t = jnp.zeros((LANES, q), F32)
    for part, part_t in zip(_split3(dta), _split3(dta_t)):
        acs = acs + _dot(tri, part)
        acs_t = acs_t + _dot(part_t, tri_t)
    last = 0 if rev else q - 1
    tot = acs[last:last + 1, :]

    d_off = SSM_HEADS if rev else 0
    hp = SSM_HEADS // 2
    ppg = hp // SSM_GROUPS
    lane = lax.broadcasted_iota(jnp.int32, (q, LANES), 1)
    for grp in range(SSM_GROUPS):
        b_g = xbc[:, SSM_INNER + grp * SSM_N:SSM_INNER + (grp + 1) * SSM_N]
        c_g = xbc[:, SSM_INNER + (SSM_GROUPS + grp) * SSM_N:SSM_INNER + (SSM_GROUPS + grp + 1) * SSM_N]
        b_bf = b_g.astype(BF16)
        c_bf = c_g.astype(BF16)
        bt_bf = b_g.T.astype(BF16)
        cb = _dot_nt(c_bf, b_bf)
        for pp in range(ppg):
            pr = grp * ppg + pp
            c0 = d_off + 2 * pr
            sl = slice(pr * LANES, (pr + 1) * LANES)
            xs_p = xbc[:, sl]
            xdt = xs_p * _pair_bcast(dtv, c0, q)
            xdt_bf = xdt.astype(BF16)
            acs_p = _pair_bcast(acs, c0, q)
            tot_p = _pair_bcast(tot, c0, 1)
            y_heads = []
            for hh in range(2):
                col = c0 + hh
                seg = acs[:, col:col + 1] - acs_t[col:col + 1, :]
                lmat = jnp.exp(jnp.where(causal, seg, NEG_INF))
                y_heads.append(_dot((cb * lmat).astype(BF16), xdt_bf))
            y_diag = jnp.where(lane < SSM_P, y_heads[0], y_heads[1])
            hs = slice(pp * LANES, (pp + 1) * LANES)
            h_p = h_ref[grp, :, hs]
            y_p = y_diag + _dot(c_bf, h_p.astype(BF16)) * jnp.exp(acs_p)
            upd = _dot(bt_bf, (xdt * jnp.exp(tot_p - acs_p)).astype(BF16))
            h_ref[grp, :, hs] = h_p * jnp.exp(tot_p) + upd
            if final:
                y_all = y_p + yo_ref[:, sl] + xs_p * dsk_ref[:, sl]
                gated_ref[:, sl] = y_all * _silu(z_ref[:, sl])
            else:
                y_ref[:, sl] = y_p
    if final:
        gt = gated_ref[...]
        ms = jnp.mean(gt * gt, axis=-1, keepdims=True)
        y_ref[...] = (gt * lax.rsqrt(ms + EPS) * ng_ref[...]).astype(y_ref.dtype)

    @pl.when(c == nc - 1)
    def _():
        hl_ref[...] = h_ref[...]


def _ssd_call(xbc, dt, z, y_other, h0, conv_w, conv_b, dt_bias128, alog128, d_skip, norm_g,
              batch, seq_len, rev):
    final = y_other is not None
    q = SSM_CHUNK
    nc = seq_len // q
    hb = q // SUBLANES
    nrow8 = batch * seq_len // SUBLANES

    def cidx(c):
        return (nc - 1 - c) if rev else c

    def main(n):
        return pl.BlockSpec((q, n), lambda b, c: (b * nc + cidx(c), 0))

    prev_spec = pl.BlockSpec((SUBLANES, SSM_XBC),
                             lambda b, c: (jnp.maximum((b * nc + cidx(c)) * hb - 1, 0), 0))
    next_spec = pl.BlockSpec((SUBLANES, SSM_XBC),
                             lambda b, c: (jnp.minimum((b * nc + cidx(c) + 1) * hb, nrow8 - 1), 0))
    state_spec = pl.BlockSpec((None, SSM_GROUPS, SSM_N, SSM_INNER // SSM_GROUPS),
                              lambda b, c: (b, 0, 0, 0))
    args = [xbc, xbc, xbc, dt, conv_w, conv_b.reshape(1, SSM_XBC), dt_bias128, alog128, h0]
    specs = [main(SSM_XBC), prev_spec, next_spec, main(LANES), _full(conv_w.shape),
             _full((1, SSM_XBC)), _full((1, LANES)), _full((1, LANES)), state_spec]
    scratch = [pltpu.VMEM((SSM_GROUPS, SSM_N, SSM_INNER // SSM_GROUPS), F32)]
    if final:
        args += [y_other, z, d_skip, norm_g.reshape(1, SSM_INNER)]
        specs += [main(SSM_INNER), main(SSM_INNER), _full((1, SSM_INNER)), _full((1, SSM_INNER))]
        scratch.append(pltpu.VMEM((q, SSM_INNER), F32))
    return pl.pallas_call(
        functools.partial(_ssd_kernel, rev=rev, final=final, nc=nc),
        grid=(batch, nc),
        in_specs=specs,
        out_specs=[main(SSM_INNER), state_spec],
        out_shape=[jax.ShapeDtypeStruct((batch * seq_len, SSM_INNER), BF16 if final else F32),
                   jax.ShapeDtypeStruct(h0.shape, F32)],
        scratch_shapes=scratch,
        compiler_params=_params("parallel", "arbitrary"),
        name="ssd_bwd_final" if final else "ssd_fwd",
    )(*args)


def _merge_kernel(x_ref, mod_ref, g_ref, wg_ref, ya_ref, ys_ref, yc_ref, woa_ref, wob_ref, woc_ref,
                  wout_ref, o_ref):
    x = x_ref[...]
    d = x.shape[-1]
    h = _norm_mod(x, g_ref[...], mod_ref[0:1, :], mod_ref[1:2, :]).astype(BF16)
    m = (_sigmoid(_dot(h, wg_ref[:, 0:d])) * _dot(ya_ref[...], woa_ref[...])
         + _sigmoid(_dot(h, wg_ref[:, d:2 * d])) * _dot(ys_ref[...], wob_ref[...])
         + _sigmoid(_dot(h, wg_ref[:, 2 * d:3 * d])) * _dot(yc_ref[...], woc_ref[...]))
    o_ref[...] = x + mod_ref[2:3, :] * _dot(m.astype(BF16), wout_ref[...])


def _merge_call(x2d, mod, g, wg, ya, ys, yc, woa, wob, woc, wout, bm, seq_len, per_seq):
    t, d = x2d.shape
    row = lambda n: pl.BlockSpec((bm, n), lambda i: (i, 0))
    return pl.pallas_call(
        _merge_kernel,
        grid=(t // bm,),
        in_specs=[row(d), _mod_spec(bm, seq_len, per_seq), _full((1, d)), _full(wg.shape),
                  row(ya.shape[1]), row(ys.shape[1]), row(yc.shape[1]),
                  _full(woa.shape), _full(wob.shape), _full(woc.shape), _full(wout.shape)],
        out_specs=row(d),
        out_shape=jax.ShapeDtypeStruct((t, d), F32),
        compiler_params=_params("parallel"),
        name="merge",
    )(x2d, mod, g.reshape(1, d), wg, ya, ys, yc, woa, wob, woc, wout)


def _ffn_kernel(*refs, bm, seq_len, final_norm, ff_chunk):
    x_ref, xp_ref, xn_ref, mod_ref, g_ref, wup_ref, wgt_ref, cw_ref, cb_ref, wdn_ref = refs[:10]
    k = 10
    if final_norm:
        fg_ref = refs[k]
        k += 1
    o_ref = refs[k]
    i = pl.program_id(0)
    x = x_ref[...]
    g = g_ref[...]
    shift, scale = mod_ref[3:4, :], mod_ref[4:5, :]
    hm = _norm_mod(x, g, shift, scale)
    h = hm.astype(BF16)
    h_ext = jnp.concatenate([_norm_mod(xp_ref[...], g, shift, scale), hm,
                             _norm_mod(xn_ref[...], g, shift, scale)], axis=0).astype(BF16)
    pos = (i * bm + lax.broadcasted_iota(jnp.int32, (bm, 1), 0)) % seq_len
    has_prev = pos > 0
    has_next = pos < seq_len - 1
    n_ext = bm + 2 * SUBLANES
    acc = jnp.zeros(x.shape, F32)
    for c0 in range(0, D_FF, ff_chunk):
        cs = slice(c0, c0 + ff_chunk)
        gt = _dot(h_ext, wgt_ref[:, cs])
        gm1 = pltpu.roll(gt, 1, 0)[SUBLANES:SUBLANES + bm, :]
        gp1 = pltpu.roll(gt, n_ext - 1, 0)[SUBLANES:SUBLANES + bm, :]
        conv = (jnp.where(has_prev, gm1, 0.0) * cw_ref[0:1, cs]
                + gt[SUBLANES:SUBLANES + bm, :] * cw_ref[1:2, cs]
                + jnp.where(has_next, gp1, 0.0) * cw_ref[2:3, cs] + cb_ref[:, cs])
        act = (_silu(conv) * _dot(h, wup_ref[:, cs])).astype(BF16)
        acc = acc + _dot(act, wdn_ref[cs, :])
    out = x + mod_ref[5:6, :] * acc
    if final_norm:
        ms = jnp.mean(out * out, axis=-1, keepdims=True)
        out = out * lax.rsqrt(ms + EPS) * fg_ref[...]
    o_ref[...] = out


def _ffn_call(x2d, mod, g, wup, wgt, conv_w, conv_b, wdn, final_g, bm, seq_len, per_seq):
    t, d = x2d.shape
    hb = bm // SUBLANES
    nrow8 = t // SUBLANES
    args = [x2d, x2d, x2d, mod, g.reshape(1, d), wup, wgt, conv_w, conv_b.reshape(1, D_FF), wdn]
    specs = [pl.BlockSpec((bm, d), lambda i: (i, 0)),
             pl.BlockSpec((SUBLANES, d), lambda i: (jnp.maximum(i * hb - 1, 0), 0)),
             pl.BlockSpec((SUBLANES, d), lambda i: (jnp.minimum((i + 1) * hb, nrow8 - 1), 0)),
             _mod_spec(bm, seq_len, per_seq), _full((1, d)), _full(wup.shape), _full(wgt.shape),
             _full(conv_w.shape), _full((1, D_FF)), _full(wdn.shape)]
    if final_g is not None:
        args.append(final_g.reshape(1, d))
        specs.append(_full((1, d)))
    return pl.pallas_call(
        functools.partial(_ffn_kernel, bm=bm, seq_len=seq_len, final_norm=final_g is not None,
                          ff_chunk=D_FF // 2),
        grid=(t // bm,),
        in_specs=specs,
        out_specs=pl.BlockSpec((bm, d), lambda i: (i, 0)),
        out_shape=jax.ShapeDtypeStruct((t, d), F32),
        compiler_params=_params("parallel"),
        name="ffn",
    )(*args)


def _rope_tables(seq_len):
    rows = seq_len // GRID_W
    t_row = jnp.repeat(jnp.arange(rows), GRID_W).astype(F32)
    t_col = jnp.tile(jnp.arange(GRID_W), rows).astype(F32)
    n = HEAD_DIM // 4
    inv = ROPE_BASE ** (-jnp.arange(n, dtype=F32) / n)
    ang = jnp.concatenate([t_row[:, None] * inv, t_col[:, None] * inv], axis=-1)
    cos = jnp.repeat(jnp.cos(ang), 2, axis=-1)
    sin = jnp.repeat(jnp.sin(ang), 2, axis=-1) * jnp.tile(jnp.array([-1.0, 1.0], F32), HEAD_DIM // 2)
    reps = LANES // HEAD_DIM
    return jnp.tile(cos, (1, reps)), jnp.tile(sin, (1, reps))


def _pad_lanes(v):
    return jnp.pad(v.reshape(1, -1), ((0, 0), (0, LANES - v.size)))


def kernel(x, c, ctx, c_ctx, w_mod, b_mod, norm1, norm2, w_in, a_sink, ssm_conv_w, ssm_conv_b,
           ssm_A_log, ssm_dt_bias, ssm_D, ssm_norm, c_q_norm, c_k_norm, w_oa, w_ob, w_oc, w_out,
           ffn_w_up, ffn_w_gate, ffn_conv_w, ffn_conv_b, ffn_w_down, final_norm):
    batch, seq, d = x.shape
    lc = ctx.shape[1]
    depth = w_mod.shape[0]
    bm_lat = min(512, seq)
    bm_ctx = min(512, batch * lc)
    bq_lat = min(256, seq)
    bq_ctx = min(256, lc)

    rope_tabs = _rope_tables(seq)
    c8 = jnp.zeros((SUBLANES, d), F32).at[:batch].set(c).at[batch].set(c_ctx)

    o_aq = 0
    o_bz = o_aq + QKV_W
    o_bx = o_bz + SSM_INNER
    o_bdt = o_bx + SSM_XBC
    o_cq = o_bdt + 2 * SSM_HEADS
    o_g = o_cq + QKV_W

    xl = x.reshape(batch * seq, d)
    xc = ctx.reshape(batch * lc, d)
    zero_state = jnp.zeros((batch, SSM_GROUPS, SSM_N, SSM_INNER // SSM_GROUPS), F32)

    for l in range(depth):
        last = l == depth - 1
        mod = _mod_call(c8, w_mod[l], b_mod[l])
        mod_lat = mod[:batch].reshape(batch, 6, d)
        mod_ctx = mod[batch:batch + 1].reshape(1, 6, d)

        wl = w_in[l]
        w_a = wl[:, o_aq:o_bz].astype(BF16)
        w_z = wl[:, o_bz:o_bx].astype(BF16)
        w_x = wl[:, o_bx:o_bdt].astype(BF16)
        w_dt = jnp.pad(wl[:, o_bdt:o_cq], ((0, 0), (0, LANES - 2 * SSM_HEADS))).astype(BF16)
        w_c = wl[:, o_cq:o_g].astype(BF16)
        w_g = wl[:, o_g:].astype(BF16)
        qk_gains = (c_q_norm[l], c_k_norm[l])
        dtb = _pad_lanes(ssm_dt_bias[l])
        alog = _pad_lanes(ssm_A_log[l])
        d_skip = jnp.repeat(ssm_D[l], SSM_P).reshape(1, SSM_INNER)

        qkv_a_c = _inproj_attn_call(xc, mod_ctx, norm1[l], w_a, bm_ctx, lc, False)
        qkv_c_c = _inproj_attn_call(xc, mod_ctx, norm1[l], w_c, bm_ctx, lc, False, None, qk_gains)
        z_c, xbc_c, dt_c = _inproj_ssm_call(xc, mod_ctx, norm1[l], w_z, w_x, w_dt, bm_ctx, lc, False)
        qkv_a = _inproj_attn_call(xl, mod_lat, norm1[l], w_a, bm_lat, seq, True, rope_tabs)
        qkv_c = _inproj_attn_call(xl, mod_lat, norm1[l], w_c, bm_lat, seq, True, rope_tabs, qk_gains)
        z_l, xbc_l, dt_l = _inproj_ssm_call(xl, mod_lat, norm1[l], w_z, w_x, w_dt, bm_lat, seq, True)

        qkv_a_c3 = qkv_a_c.reshape(batch, lc, QKV_W)
        qkv_c_c3 = qkv_c_c.reshape(batch, lc, QKV_W)
        qkv_a3 = qkv_a.reshape(batch, seq, QKV_W)
        qkv_c3 = qkv_c.reshape(batch, seq, QKV_W)

        ya = _attn_call(qkv_a3, [qkv_a_c3], qkv_a3, a_sink[l], bq_lat).reshape(batch * seq, -1)
        yc = _attn_call(qkv_c3, [qkv_c3, qkv_c_c3], None, None, bq_lat).reshape(batch * seq, -1)

        ssm_args = (ssm_conv_w[l], ssm_conv_b[l], dtb, alog, d_skip, ssm_norm[l])
        yf_c, hf_c = _ssd_call(xbc_c, dt_c, None, None, zero_state, *ssm_args, batch, lc, False)
        ys_c, hb_c = _ssd_call(xbc_c, dt_c, z_c, yf_c, zero_state, *ssm_args, batch, lc, True)
        yf, _ = _ssd_call(xbc_l, dt_l, None, None, hf_c, *ssm_args, batch, seq, False)
        ys, _ = _ssd_call(xbc_l, dt_l, z_l, yf, hb_c, *ssm_args, batch, seq, True)

        wo = (w_oa[l].astype(BF16), w_ob[l].astype(BF16), w_oc[l].astype(BF16), w_out[l].astype(BF16))
        wf = (ffn_w_up[l].astype(BF16), ffn_w_gate[l].astype(BF16), ffn_conv_w[l], ffn_conv_b[l],
              ffn_w_down[l].astype(BF16))
        xl = _merge_call(xl, mod_lat, norm1[l], w_g, ya, ys, yc, *wo, bm_lat, seq, True)
        xl = _ffn_call(xl, mod_lat, norm2[l], *wf, final_norm if last else None, bm_lat, seq, True)

        if not last:
            ya_c = _attn_call(qkv_a_c3, [qkv_a_c3], None, a_sink[l], bq_ctx).reshape(batch * lc, -1)
            yc_c = _attn_call(qkv_c_c3, [qkv_c_c3], None, None, bq_ctx).reshape(batch * lc, -1)
            xc = _merge_call(xc, mod_ctx, norm1[l], w_g, ya_c, ys_c, yc_c, *wo, bm_ctx, lc, False)
            xc = _ffn_call(xc, mod_ctx, norm2[l], *wf, None, bm_ctx, lc, False)

    return xl.reshape(batch, seq, d)
```

```python
import functools
import math

import jax
import jax.numpy as jnp
from jax import lax
from jax.experimental import pallas as pl
from jax.experimental.pallas import tpu as pltpu

F32 = jnp.float32
BF16 = jnp.bfloat16

D_MODEL = 1024
GRID_W = 64
HEAD_DIM = 64
ROPE_BASE = 10000.0
EPS = 1e-6
WINDOW = 128
N_HEADS = 8
N_KV = 2
QKV_W = (N_HEADS + 2 * N_KV) * HEAD_DIM
QK_W = N_HEADS * HEAD_DIM + N_KV * 128
LOG2E = math.log2(math.e)
Q_SCALE = HEAD_DIM ** -0.5 * LOG2E
ATTN_KEY_TILE = 256
ATTN_SCORE_SLOTS = 3
SSM_HEADS = 16
SSM_P = 64
SSM_INNER = SSM_HEADS * SSM_P
SSM_GROUPS = 2
SSM_N = 128
SSM_CHUNK = 128
SSD_CHUNKS_PER_STEP = 4
SSM_XBC = SSM_INNER + 2 * SSM_GROUPS * SSM_N
D_FF = 2816

LANES = 128
SUBLANES = 8
VMEM_LIMIT_BYTES = 56 * 1024 * 1024

NEG_INF = float("-inf")


def _params(*sem):
    return pltpu.CompilerParams(dimension_semantics=sem, vmem_limit_bytes=VMEM_LIMIT_BYTES)


def _full(shape):
    n = len(shape)
    return pl.BlockSpec(shape, lambda *_: (0,) * n)


def _sigmoid(x):
    return 1.0 / (1.0 + jnp.exp(-x))


def _silu(x):
    return x * _sigmoid(x)


def _norm_mod(x, g, shift, scale):
    ms = jnp.mean(x * x, axis=-1, keepdims=True)
    return (x * lax.rsqrt(ms + EPS) * g) * (1.0 + scale) + shift


def _dot(a, b):
    return jnp.dot(a, b, preferred_element_type=F32)


def _dot_nt(a, b):
    return lax.dot_general(a, b, (((1,), (1,)), ((), ())), preferred_element_type=F32)


def _split3(x):
    hi = x.astype(BF16)
    r1 = x - hi.astype(F32)
    mid = r1.astype(BF16)
    lo = (r1 - mid.astype(F32)).astype(BF16)
    return hi, mid, lo


def _mod_kernel(c_ref, w_ref, b_ref, o_ref):
    s = _silu(c_ref[...])
    o_ref[...] = jnp.dot(s, w_ref[...], preferred_element_type=F32,
                         precision=lax.Precision.HIGHEST) + b_ref[...]


def _mod_call(c8, w, b):
    d, n = w.shape
    bn = 1536
    return pl.pallas_call(
        _mod_kernel,
        grid=(n // bn,),
        in_specs=[_full((SUBLANES, d)),
                  pl.BlockSpec((d, bn), lambda j: (0, j)),
                  pl.BlockSpec((1, bn), lambda j: (0, j))],
        out_specs=pl.BlockSpec((SUBLANES, bn), lambda j: (0, j)),
        out_shape=jax.ShapeDtypeStruct((SUBLANES, n), F32),
        compiler_params=_params("arbitrary"),
        name="mod_proj",
    )(c8, w, b.reshape(1, n))


def _mod_spec(bm, seq_len, per_seq):
    if per_seq:
        return pl.BlockSpec((None, 6, D_MODEL), lambda i: ((i * bm) // seq_len, 0, 0))
    return pl.BlockSpec((None, 6, D_MODEL), lambda i: (0, 0, 0))


def _rope(y, cos, sin_signed):
    lane = lax.broadcasted_iota(jnp.int32, y.shape, 1)
    swapped = jnp.where((lane & 1) == 0, pltpu.roll(y, LANES - 1, 1), pltpu.roll(y, 1, 1))
    return y * cos + swapped * sin_signed


def _inproj_attn_kernel(*refs, rope, qknorm, v_transposed):
    x_ref, mod_ref, g_ref, w_ref = refs[:4]
    k = 4
    if rope:
        cos_ref, sin_ref = refs[k:k + 2]
        k += 2
    if qknorm:
        qg_ref, kg_ref, pool_ref = refs[k:k + 3]
        k += 3
    qk_ref, v_ref = refs[k:k + 2]
    h = _norm_mod(x_ref[...], g_ref[...], mod_ref[0:1, :], mod_ref[1:2, :]).astype(BF16)
    u = _dot(h, w_ref[...])
    n_q = N_HEADS * HEAD_DIM // LANES
    lane = lax.broadcasted_iota(jnp.int32, (u.shape[0], LANES), 1)
    low = lane < HEAD_DIM
    for t in range(n_q + 2):
        y = u[:, t * LANES:(t + 1) * LANES]
        if t <= n_q:
            if qknorm:
                sq = y * y
                hi = sq.astype(BF16)
                lo = (sq - hi.astype(F32)).astype(BF16)
                ss = _dot(hi, pool_ref[...]) + _dot(lo, pool_ref[...])
                gain = qg_ref[...] if t < n_q else kg_ref[...]
                y = y * lax.rsqrt(ss * (1.0 / HEAD_DIM) + EPS) * gain
            if rope:
                y = _rope(y, cos_ref[...], sin_ref[...])
        if t < n_q:
            qk_ref[:, t * LANES:(t + 1) * LANES] = (y * Q_SCALE).astype(BF16)
        elif t == n_q:
            qk_ref[:, t * LANES:(t + 1) * LANES] = jnp.where(low, y, 0.0).astype(BF16)
            qk_ref[:, (t + 1) * LANES:(t + 2) * LANES] = jnp.where(
                low, pltpu.roll(y, HEAD_DIM, 1), 0.0).astype(BF16)
        else:
            ones_col = jnp.where(lane == HEAD_DIM, 1.0, 0.0)
            v0 = jnp.where(low, y, ones_col)
            v1 = jnp.where(low, pltpu.roll(y, HEAD_DIM, 1), ones_col)
            if v_transposed:
                v_ref[0] = v0.T.astype(BF16)
                v_ref[1] = v1.T.astype(BF16)
            else:
                v_ref[:, 0:LANES] = v0.astype(BF16)
                v_ref[:, LANES:2 * LANES] = v1.astype(BF16)


def _inproj_attn_call(x2d, mod, g, w, bm, seq_len, per_seq, v_transposed, rope_tabs=None, qk_gains=None):
    t, d = x2d.shape
    rope = rope_tabs is not None
    qknorm = qk_gains is not None
    args = [x2d, mod, g.reshape(1, d), w]
    specs = [pl.BlockSpec((bm, d), lambda i: (i, 0)), _mod_spec(bm, seq_len, per_seq),
             _full((1, d)), _full((d, QKV_W))]
    if rope:
        nblk = seq_len // bm
        args += list(rope_tabs)
        specs += [pl.BlockSpec((bm, LANES), lambda i: (i % nblk, 0))] * 2
    if qknorm:
        qg, kg = qk_gains
        lane_head = jnp.arange(LANES) // HEAD_DIM
        pool = (lane_head[:, None] == lane_head[None, :]).astype(BF16)
        args += [jnp.tile(qg, LANES // HEAD_DIM).reshape(1, LANES),
                 jnp.tile(kg, LANES // HEAD_DIM).reshape(1, LANES), pool]
        specs += [_full((1, LANES)), _full((1, LANES)), _full((LANES, LANES))]
    return pl.pallas_call(
        functools.partial(_inproj_attn_kernel, rope=rope, qknorm=qknorm, v_transposed=v_transposed),
        grid=(t // bm,),
        in_specs=specs,
        out_specs=[pl.BlockSpec((bm, QK_W), lambda i: (i, 0)),
                   pl.BlockSpec((N_KV, LANES, bm), lambda i: (0, 0, i)) if v_transposed
                   else pl.BlockSpec((bm, N_KV * LANES), lambda i: (i, 0))],
        out_shape=[jax.ShapeDtypeStruct((t, QK_W), BF16),
                   jax.ShapeDtypeStruct((N_KV, LANES, t) if v_transposed else (t, N_KV * LANES), BF16)],
        compiler_params=_params("parallel"),
        name="inproj_attn",
    )(*args)


def _inproj_ssm_kernel(x_ref, mod_ref, g_ref, wz_ref, wx_ref, wdt_ref, z_ref, xbc_ref, dt_ref):
    h = _norm_mod(x_ref[...], g_ref[...], mod_ref[0:1, :], mod_ref[1:2, :]).astype(BF16)
    z_ref[...] = _dot(h, wz_ref[...])
    xbc_ref[...] = _dot(h, wx_ref[...])
    dt_ref[...] = _dot(h, wdt_ref[...])


def _inproj_ssm_call(x2d, mod, g, wz, wx, wdt, bm, seq_len, per_seq):
    t, d = x2d.shape
    row = lambda n: pl.BlockSpec((bm, n), lambda i: (i, 0))
    return pl.pallas_call(
        _inproj_ssm_kernel,
        grid=(t // bm,),
        in_specs=[row(d), _mod_spec(bm, seq_len, per_seq), _full((1, d)),
                  _full((d, SSM_INNER)), _full((d, SSM_XBC)), _full((d, LANES))],
        out_specs=[row(SSM_INNER), row(SSM_XBC), row(LANES)],
        out_shape=[jax.ShapeDtypeStruct((t, SSM_INNER), F32),
                   jax.ShapeDtypeStruct((t, SSM_XBC), F32),
                   jax.ShapeDtypeStruct((t, LANES), F32)],
        compiler_params=_params("parallel"),
        name="inproj_ssm",
    )(x2d, mod, g.reshape(1, d), wz, wx, wdt)


def _attn_t_kernel(*refs, bq, dense_lens, win_len, has_sink):
    q_ref = refs[0]
    k = 1
    segs = []
    for _ in dense_lens:
        segs.append((refs[k:k + 2], refs[k + 2:k + 4]))
        k += 4
    if win_len:
        win_k, win_v = refs[k:k + 2], refs[k + 2:k + 4]
        k += 4
    if has_sink:
        sink_ref = refs[k]
        k += 1
    o_ref, s_ref = refs[k:k + 2]
    i = pl.program_id(1)
    tk = ATTN_KEY_TILE

    tiles = [("dense", j, o) for j, lk in enumerate(dense_lens) for o in range(0, lk, tk)]
    if win_len:
        w = bq + 2 * WINDOW
        start = pl.multiple_of(jnp.clip(i * bq - WINDOW, 0, win_len - w), LANES)
        tiles += [("win", 0, o) for o in range(0, w, tk)]
        q_pos = i * bq + lax.broadcasted_iota(jnp.int32, (tk, bq), 1)
        k_iota = lax.broadcasted_iota(jnp.int32, (tk, bq), 0)

    def k_tile(g, t):
        kind, j, o = t
        if kind == "dense":
            return segs[j][0][g][o:o + tk, :]
        return win_k[g][pl.ds(pl.multiple_of(start + o, LANES), tk), :]

    def v_tile(g, t):
        kind, j, o = t
        if kind == "dense":
            return segs[j][1][g][:, o:o + tk]
        return win_v[g][:, pl.ds(pl.multiple_of(start + o, LANES), tk)]

    def head_q(h):
        tile = q_ref[:, (h // 2) * LANES:(h // 2 + 1) * LANES]
        return tile if h % 2 == 0 else pltpu.roll(tile.astype(F32), HEAD_DIM, 1).astype(BF16)

    def score_tile(h, qh, n, t, macc):
        s = _dot_nt(k_tile(h // hpk, t), qh)
        if t[0] == "win":
            s = jnp.where(jnp.abs(q_pos - (start + t[2] + k_iota)) <= WINDOW, s, NEG_INF)
        s_ref[h % 2, n * tk:(n + 1) * tk, :] = s
        mt = s.reshape(tk // SUBLANES, SUBLANES, bq).max(axis=0)
        return mt if macc is None else jnp.maximum(macc, mt)

    def finish_max(h, macc):
        m = macc.max(axis=0, keepdims=True)
        if has_sink:
            m = jnp.maximum(m, sink_ref[0, h] * LOG2E)
        return m

    hpk = N_HEADS // N_KV
    row = lax.broadcasted_iota(jnp.int32, (LANES, bq), 0)
    qh = head_q(0)
    macc = None
    for n, t in enumerate(tiles):
        macc = score_tile(0, qh, n, t, macc)
    m = finish_max(0, macc)
    o_even = None
    for h in range(N_HEADS):
        nxt = h + 1 < N_HEADS
        if nxt:
            qn = head_q(h + 1)
        macc = None
        acc = jnp.zeros((LANES, bq), F32)
        for n, t in enumerate(tiles):
            if nxt:
                macc = score_tile(h + 1, qn, n, t, macc)
            p = jnp.exp2(s_ref[h % 2, n * tk:(n + 1) * tk, :] - m).astype(BF16)
            acc = acc + _dot(v_tile(h // hpk, t), p)
        l = acc[HEAD_DIM:HEAD_DIM + 1, :]
        if has_sink:
            l = l + jnp.exp2(sink_ref[0, h] * LOG2E - m)
        o = acc / l
        if h % 2 == 0:
            o_even = o
        else:
            pair = jnp.where(row < HEAD_DIM, o_even, pltpu.roll(o, HEAD_DIM, 0))
            o_ref[:, (h // 2) * LANES:(h // 2 + 1) * LANES] = pair.T.astype(o_ref.dtype)
        if nxt:
            m = finish_max(h + 1, macc)


def _attn_rows_kernel(*refs, bq, dense_lens, has_sink):
    q_ref = refs[0]
    k = 1
    segs = []
    for lk in dense_lens:
        segs.append((refs[k:k + 2], refs[k + 2:k + 4], lk))
        k += 4
    if has_sink:
        sink_ref = refs[k]
        k += 1
    o_ref, s_ref = refs[k:k + 2]
    n_slots = s_ref.shape[0]
    lane = lax.broadcasted_iota(jnp.int32, (bq, LANES), 1)
    hpk = N_HEADS // N_KV
    o_even = None
    for h in range(N_HEADS):
        g = h // hpk
        slot = h % n_slots
        tile = q_ref[:, (h // 2) * LANES:(h // 2 + 1) * LANES]
        qh = tile if h % 2 == 0 else pltpu.roll(tile.astype(F32), HEAD_DIM, 1).astype(BF16)
        off = 0
        for k_refs, _, lk in segs:
            s_ref[slot, :, off:off + lk] = _dot_nt(qh, k_refs[g][...])
            off += lk
        s = s_ref[slot]
        m = s.max(axis=-1, keepdims=True)
        if has_sink:
            sink = sink_ref[0, h] * LOG2E
            m = jnp.maximum(m, sink)
        p = jnp.exp2(s - m).astype(BF16)
        acc = jnp.zeros((bq, LANES), F32)
        off = 0
        for _, v_refs, lk in segs:
            acc = acc + _dot(p[:, off:off + lk], v_refs[g][...])
            off += lk
        l = acc[:, HEAD_DIM:HEAD_DIM + 1]
        if has_sink:
            l = l + jnp.exp2(sink - m)
        o = acc / l
        if h % 2 == 0:
            o_even = o
        else:
            pair = jnp.where(lane < HEAD_DIM, o_even, pltpu.roll(o, HEAD_DIM, 1))
            o_ref[:, (h // 2) * LANES:(h // 2 + 1) * LANES] = pair.astype(o_ref.dtype)


def _attn_call(q_arr, dense_kv, win_kv, sink, bq, transposed):
    b, lq, _ = q_arr.shape
    q_w = N_HEADS * HEAD_DIM
    k_blk = q_w // LANES
    args = [q_arr]
    specs = [pl.BlockSpec((None, bq, q_w), lambda bi, i: (bi, i, 0))]

    def kv_specs(qk, v):
        lk = qk.shape[1]
        ks = [pl.BlockSpec((None, lk, LANES), functools.partial(lambda bi, i, j: (bi, 0, j), j=k_blk + j))
              for j in range(N_KV)]
        if transposed:
            vs = [pl.BlockSpec((None, LANES, lk), functools.partial(lambda bi, i, j: (j, 0, bi), j=j))
                  for j in range(N_KV)]
        else:
            vs = [pl.BlockSpec((None, lk, LANES), functools.partial(lambda bi, i, j: (bi, 0, j), j=j))
                  for j in range(N_KV)]
        return [qk] * N_KV + [v] * N_KV, ks + vs

    for qk, v in dense_kv:
        a, s = kv_specs(qk, v)
        args += a
        specs += s
    dense_lens = tuple(qk.shape[1] for qk, _ in dense_kv)
    n_keys = sum(dense_lens)
    win_len = 0
    if win_kv is not None:
        assert transposed
        win_len = win_kv[0].shape[1]
        assert win_len >= bq + 2 * WINDOW
        n_keys += bq + 2 * WINDOW
        a, s = kv_specs(*win_kv)
        args += a
        specs += s
    if sink is not None:
        args.append(sink.reshape(1, N_HEADS))
        specs.append(pl.BlockSpec(memory_space=pltpu.SMEM))
    if transposed:
        body = functools.partial(_attn_t_kernel, bq=bq, dense_lens=dense_lens, win_len=win_len,
                                 has_sink=sink is not None)
        scratch = pltpu.VMEM((2, n_keys, bq), F32)
    else:
        body = functools.partial(_attn_rows_kernel, bq=bq, dense_lens=dense_lens, has_sink=sink is not None)
        scratch = pltpu.VMEM((ATTN_SCORE_SLOTS, bq, n_keys), F32)
    return pl.pallas_call(
        body,
        grid=(b, lq // bq),
        in_specs=specs,
        out_specs=pl.BlockSpec((None, bq, q_w), lambda bi, i: (bi, i, 0)),
        out_shape=jax.ShapeDtypeStruct((b, lq, q_w), BF16),
        scratch_shapes=[scratch],
        compiler_params=_params("parallel", "parallel"),
        name="attn_t" if transposed else "attn_rows",
    )(*args)


def _pair_bcast(arr, c0, rows):
    lane = lax.broadcasted_iota(jnp.int32, (rows, LANES), 1)
    return jnp.where(lane < SSM_P, arr[:, c0:c0 + 1], arr[:, c0 + 1:c0 + 2])


def _ssd_kernel(*refs, rev, final, nb):
    (x_ref, xp_ref, xn_ref, dt_ref, cw_ref, cb_ref, dtb_ref, alog_ref, h0_ref) = refs[:9]
    k = 9
    if final:
        yo_ref, z_ref, dsk_ref, ng_ref = refs[k:k + 4]
        k += 4
    y_ref, hl_ref, h_ref = refs[k:k + 3]
    if final:
        gated_ref = refs[k + 3]

    c = pl.program_id(1)
    bidx = (nb - 1 - c) if rev else c
    q = SSM_CHUNK
    rows_blk = x_ref.shape[0]

    @pl.when(c == 0)
    def _():
        h_ref[...] = h0_ref[...]

    x = x_ref[...]
    prev = jnp.where(bidx > 0, xp_ref[SUBLANES - 1:SUBLANES, :], 0.0)
    nxt = jnp.where(bidx < nb - 1, xn_ref[0:1, :], 0.0)
    row = lax.broadcasted_iota(jnp.int32, x.shape, 0)
    xm1 = jnp.where(row == 0, prev, pltpu.roll(x, 1, 0))
    xp1 = jnp.where(row == rows_blk - 1, nxt, pltpu.roll(x, rows_blk - 1, 0))
    xbc_blk = _silu(xm1 * cw_ref[0:1, :] + x * cw_ref[1:2, :] + xp1 * cw_ref[2:3, :] + cb_ref[...])

    ri = lax.broadcasted_iota(jnp.int32, (q, q), 0)
    ci = lax.broadcasted_iota(jnp.int32, (q, q), 1)
    causal = (ci >= ri) if rev else (ci <= ri)
    tri = causal.astype(BF16)
    tri_t = ((ri >= ci) if rev else (ri <= ci)).astype(BF16)
    d_off = SSM_HEADS if rev else 0
    hp = SSM_HEADS // 2
    ppg = hp // SSM_GROUPS
    lane = lax.broadcasted_iota(jnp.int32, (q, LANES), 1)
    last = 0 if rev else q - 1

    n_sub = rows_blk // q
    for sub in (range(n_sub - 1, -1, -1) if rev else range(n_sub)):
        rs = slice(sub * q, (sub + 1) * q)
        xbc = xbc_blk[rs, :]
        dtr = dt_ref[rs, :] + dtb_ref[...]
        dtv = jnp.maximum(dtr, 0.0) + jnp.log(1.0 + jnp.exp(-jnp.abs(dtr)))
        dta = dtv * (-jnp.exp(alog_ref[...]))
        dta_t = dta.T
        acs = jnp.zeros((q, LANES), F32)
        acs_t = jnp.zeros((LANES, q), F32)
        for part, part_t in zip(_split3(dta), _split3(dta_t)):
            acs = acs + _dot(tri, part)
            acs_t = acs_t + _dot(part_t, tri_t)
        tot = acs[last:last + 1, :]

        for grp in range(SSM_GROUPS):
            b_g = xbc[:, SSM_INNER + grp * SSM_N:SSM_INNER + (grp + 1) * SSM_N]
            c_g = xbc[:, SSM_INNER + (SSM_GROUPS + grp) * SSM_N:SSM_INNER + (SSM_GROUPS + grp + 1) * SSM_N]
            b_bf = b_g.astype(BF16)
            c_bf = c_g.astype(BF16)
            bt_bf = b_g.T.astype(BF16)
            cb = _dot_nt(c_bf, b_bf)
            for pp in range(ppg):
                pr = grp * ppg + pp
                c0 = d_off + 2 * pr
                sl = slice(pr * LANES, (pr + 1) * LANES)
                xs_p = xbc[:, sl]
                xdt = xs_p * _pair_bcast(dtv, c0, q)
                xdt_bf = xdt.astype(BF16)
                acs_p = _pair_bcast(acs, c0, q)
                tot_p = _pair_bcast(tot, c0, 1)
                y_heads = []
                for hh in range(2):
                    col = c0 + hh
                    seg = acs[:, col:col + 1] - acs_t[col:col + 1, :]
                    lmat = jnp.exp(jnp.where(causal, seg, NEG_INF))
                    y_heads.append(_dot((cb * lmat).astype(BF16), xdt_bf))
                y_diag = jnp.where(lane < SSM_P, y_heads[0], y_heads[1])
                hs = slice(pp * LANES, (pp + 1) * LANES)
                h_p = h_ref[grp, :, hs]
                y_p = y_diag + _dot(c_bf, h_p.astype(BF16)) * jnp.exp(acs_p)
                upd = _dot(bt_bf, (xdt * jnp.exp(tot_p - acs_p)).astype(BF16))
                h_ref[grp, :, hs] = h_p * jnp.exp(tot_p) + upd
                if final:
                    y_all = y_p + yo_ref[rs, sl] + xs_p * dsk_ref[:, sl]
                    gated_ref[rs, sl] = y_all * _silu(z_ref[rs, sl])
                else:
                    y_ref[rs, sl] = y_p
    if final:
        gt = gated_ref[...]
        ms = jnp.mean(gt * gt, axis=-1, keepdims=True)
        y_ref[...] = (gt * lax.rsqrt(ms + EPS) * ng_ref[...]).astype(y_ref.dtype)

    @pl.when(c == nb - 1)
    def _():
        hl_ref[...] = h_ref[...]


def _ssd_call(xbc, dt, z, y_other, h0, conv_w, conv_b, dt_bias128, alog128, d_skip, norm_g,
              batch, seq_len, rev):
    final = y_other is not None
    rows_blk = min(SSD_CHUNKS_PER_STEP * SSM_CHUNK, seq_len)
    nb = seq_len // rows_blk
    hb = rows_blk // SUBLANES
    nrow8 = batch * seq_len // SUBLANES

    def bidx(c):
        return (nb - 1 - c) if rev else c

    def main(n):
        return pl.BlockSpec((rows_blk, n), lambda b, c: (b * nb + bidx(c), 0))

    prev_spec = pl.BlockSpec((SUBLANES, SSM_XBC),
                             lambda b, c: (jnp.maximum((b * nb + bidx(c)) * hb - 1, 0), 0))
    next_spec = pl.BlockSpec((SUBLANES, SSM_XBC),
                             lambda b, c: (jnp.minimum((b * nb + bidx(c) + 1) * hb, nrow8 - 1), 0))
    state_spec = pl.BlockSpec((None, SSM_GROUPS, SSM_N, SSM_INNER // SSM_GROUPS),
                              lambda b, c: (b, 0, 0, 0))
    args = [xbc, xbc, xbc, dt, conv_w, conv_b.reshape(1, SSM_XBC), dt_bias128, alog128, h0]
    specs = [main(SSM_XBC), prev_spec, next_spec, main(LANES), _full(conv_w.shape),
             _full((1, SSM_XBC)), _full((1, LANES)), _full((1, LANES)), state_spec]
    scratch = [pltpu.VMEM((SSM_GROUPS, SSM_N, SSM_INNER // SSM_GROUPS), F32)]
    if final:
        args += [y_other, z, d_skip, norm_g.reshape(1, SSM_INNER)]
        specs += [main(SSM_INNER), main(SSM_INNER), _full((1, SSM_INNER)), _full((1, SSM_INNER))]
        scratch.append(pltpu.VMEM((rows_blk, SSM_INNER), F32))
    return pl.pallas_call(
        functools.partial(_ssd_kernel, rev=rev, final=final, nb=nb),
        grid=(batch, nb),
        in_specs=specs,
        out_specs=[main(SSM_INNER), state_spec],
        out_shape=[jax.ShapeDtypeStruct((batch * seq_len, SSM_INNER), BF16 if final else F32),
                   jax.ShapeDtypeStruct(h0.shape, F32)],
        scratch_shapes=scratch,
        compiler_params=_params("parallel", "arbitrary"),
        name="ssd_bwd_final" if final else "ssd_fwd",
    )(*args)


def _merge_kernel(x_ref, mod_ref, g_ref, wg_ref, ya_ref, ys_ref, yc_ref, woa_ref, wob_ref, woc_ref,
                  wout_ref, o_ref):
    x = x_ref[...]
    d = x.shape[-1]
    h = _norm_mod(x, g_ref[...], mod_ref[0:1, :], mod_ref[1:2, :]).astype(BF16)
    m = (_sigmoid(_dot(h, wg_ref[:, 0:d])) * _dot(ya_ref[...], woa_ref[...])
         + _sigmoid(_dot(h, wg_ref[:, d:2 * d])) * _dot(ys_ref[...], wob_ref[...])
         + _sigmoid(_dot(h, wg_ref[:, 2 * d:3 * d])) * _dot(yc_ref[...], woc_ref[...]))
    o_ref[...] = x + mod_ref[2:3, :] * _dot(m.astype(BF16), wout_ref[...])


def _merge_call(x2d, mod, g, wg, ya, ys, yc, woa, wob, woc, wout, bm, seq_len, per_seq):
    t, d = x2d.shape
    row = lambda n: pl.BlockSpec((bm, n), lambda i: (i, 0))
    return pl.pallas_call(
        _merge_kernel,
        grid=(t // bm,),
        in_specs=[row(d), _mod_spec(bm, seq_len, per_seq), _full((1, d)), _full(wg.shape),
                  row(ya.shape[1]), row(ys.shape[1]), row(yc.shape[1]),
                  _full(woa.shape), _full(wob.shape), _full(woc.shape), _full(wout.shape)],
        out_specs=row(d),
        out_shape=jax.ShapeDtypeStruct((t, d), F32),
        compiler_params=_params("parallel"),
        name="merge",
    )(x2d, mod, g.reshape(1, d), wg, ya, ys, yc, woa, wob, woc, wout)


def _ffn_kernel(*refs, bm, seq_len, final_norm, ff_chunk):
    x_ref, xp_ref, xn_ref, mod_ref, g_ref, wup_ref, wgt_ref, cw_ref, cb_ref, wdn_ref = refs[:10]
    k = 10
    if final_norm:
        fg_ref = refs[k]
        k += 1
    o_ref = refs[k]
    i = pl.program_id(0)
    x = x_ref[...]
    g = g_ref[...]
    shift, scale = mod_ref[3:4, :], mod_ref[4:5, :]
    hm = _norm_mod(x, g, shift, scale)
    h = hm.astype(BF16)
    h_ext = jnp.concatenate([_norm_mod(xp_ref[...], g, shift, scale), hm,
                             _norm_mod(xn_ref[...], g, shift, scale)], axis=0).astype(BF16)
    pos = (i * bm + lax.broadcasted_iota(jnp.int32, (bm, 1), 0)) % seq_len
    has_prev = pos > 0
    has_next = pos < seq_len - 1
    n_ext = bm + 2 * SUBLANES
    acc = jnp.zeros(x.shape, F32)
    for c0 in range(0, D_FF, ff_chunk):
        cs = slice(c0, c0 + ff_chunk)
        gt = _dot(h_ext, wgt_ref[:, cs])
        gm1 = pltpu.roll(gt, 1, 0)[SUBLANES:SUBLANES + bm, :]
        gp1 = pltpu.roll(gt, n_ext - 1, 0)[SUBLANES:SUBLANES + bm, :]
        conv = (jnp.where(has_prev, gm1, 0.0) * cw_ref[0:1, cs]
                + gt[SUBLANES:SUBLANES + bm, :] * cw_ref[1:2, cs]
                + jnp.where(has_next, gp1, 0.0) * cw_ref[2:3, cs] + cb_ref[:, cs])
        act = (_silu(conv) * _dot(h, wup_ref[:, cs])).astype(BF16)
        acc = acc + _dot(act, wdn_ref[cs, :])
    out = x + mod_ref[5:6, :] * acc
    if final_norm:
        ms = jnp.mean(out * out, axis=-1, keepdims=True)
        out = out * lax.rsqrt(ms + EPS) * fg_ref[...]
    o_ref[...] = out


def _ffn_call(x2d, mod, g, wup, wgt, conv_w, conv_b, wdn, final_g, bm, seq_len, per_seq):
    t, d = x2d.shape
    hb = bm // SUBLANES
    nrow8 = t // SUBLANES
    args = [x2d, x2d, x2d, mod, g.reshape(1, d), wup, wgt, conv_w, conv_b.reshape(1, D_FF), wdn]
    specs = [pl.BlockSpec((bm, d), lambda i: (i, 0)),
             pl.BlockSpec((SUBLANES, d), lambda i: (jnp.maximum(i * hb - 1, 0), 0)),
             pl.BlockSpec((SUBLANES, d), lambda i: (jnp.minimum((i + 1) * hb, nrow8 - 1), 0)),
             _mod_spec(bm, seq_len, per_seq), _full((1, d)), _full(wup.shape), _full(wgt.shape),
             _full(conv_w.shape), _full((1, D_FF)), _full(wdn.shape)]
    if final_g is not None:
        args.append(final_g.reshape(1, d))
        specs.append(_full((1, d)))
    return pl.pallas_call(
        functools.partial(_ffn_kernel, bm=bm, seq_len=seq_len, final_norm=final_g is not None,
                          ff_chunk=D_FF // 2),
        grid=(t // bm,),
        in_specs=specs,
        out_specs=pl.BlockSpec((bm, d), lambda i: (i, 0)),
        out_shape=jax.ShapeDtypeStruct((t, d), F32),
        compiler_params=_params("parallel"),
        name="ffn",
    )(*args)


def _rope_tables(seq_len):
    rows = seq_len // GRID_W
    t_row = jnp.repeat(jnp.arange(rows), GRID_W).astype(F32)
    t_col = jnp.tile(jnp.arange(GRID_W), rows).astype(F32)
    n = HEAD_DIM // 4
    inv = ROPE_BASE ** (-jnp.arange(n, dtype=F32) / n)
    ang = jnp.concatenate([t_row[:, None] * inv, t_col[:, None] * inv], axis=-1)
    cos = jnp.repeat(jnp.cos(ang), 2, axis=-1)
    sin = jnp.repeat(jnp.sin(ang), 2, axis=-1) * jnp.tile(jnp.array([-1.0, 1.0], F32), HEAD_DIM // 2)
    reps = LANES // HEAD_DIM
    return jnp.tile(cos, (1, reps)), jnp.tile(sin, (1, reps))


def _pad_lanes(v):
    return jnp.pad(v.reshape(1, -1), ((0, 0), (0, LANES - v.size)))


def kernel(x, c, ctx, c_ctx, w_mod, b_mod, norm1, norm2, w_in, a_sink, ssm_conv_w, ssm_conv_b,
           ssm_A_log, ssm_dt_bias, ssm_D, ssm_norm, c_q_norm, c_k_norm, w_oa, w_ob, w_oc, w_out,
           ffn_w_up, ffn_w_gate, ffn_conv_w, ffn_conv_b, ffn_w_down, final_norm):
    batch, seq, d = x.shape
    lc = ctx.shape[1]
    depth = w_mod.shape[0]
    bm_lat = min(512, seq)
    bm_ctx = min(512, batch * lc)
    bq_lat = min(256, seq)
    bq_ctx = min(256, lc)

    rope_tabs = _rope_tables(seq)
    c8 = jnp.zeros((SUBLANES, d), F32).at[:batch].set(c).at[batch].set(c_ctx)

    o_aq = 0
    o_bz = o_aq + QKV_W
    o_bx = o_bz + SSM_INNER
    o_bdt = o_bx + SSM_XBC
    o_cq = o_bdt + 2 * SSM_HEADS
    o_g = o_cq + QKV_W

    xl = x.reshape(batch * seq, d)
    xc = ctx.reshape(batch * lc, d)
    zero_state = jnp.zeros((batch, SSM_GROUPS, SSM_N, SSM_INNER // SSM_GROUPS), F32)

    for l in range(depth):
        last = l == depth - 1
        mod = _mod_call(c8, w_mod[l], b_mod[l])
        mod_lat = mod[:batch].reshape(batch, 6, d)
        mod_ctx = mod[batch:batch + 1].reshape(1, 6, d)

        wl = w_in[l]
        w_a = wl[:, o_aq:o_bz].astype(BF16)
        w_z = wl[:, o_bz:o_bx].astype(BF16)
        w_x = wl[:, o_bx:o_bdt].astype(BF16)
        w_dt = jnp.pad(wl[:, o_bdt:o_cq], ((0, 0), (0, LANES - 2 * SSM_HEADS))).astype(BF16)
        w_c = wl[:, o_cq:o_g].astype(BF16)
        w_g = wl[:, o_g:].astype(BF16)
        qk_gains = (c_q_norm[l], c_k_norm[l])
        dtb = _pad_lanes(ssm_dt_bias[l])
        alog = _pad_lanes(ssm_A_log[l])
        d_skip = jnp.repeat(ssm_D[l], SSM_P).reshape(1, SSM_INNER)

        ip_c = (xc, mod_ctx, norm1[l])
        ip_l = (xl, mod_lat, norm1[l])
        qk_a_c, v_a_c = _inproj_attn_call(*ip_c, w_a, bm_ctx, lc, False, True)
        qk_c_c, v_c_c = _inproj_attn_call(*ip_c, w_c, bm_ctx, lc, False, False, None, qk_gains)
        z_c, xbc_c, dt_c = _inproj_ssm_call(*ip_c, w_z, w_x, w_dt, bm_ctx, lc, False)
        qk_a, v_a = _inproj_attn_call(*ip_l, w_a, bm_lat, seq, True, True, rope_tabs)
        qk_c, v_c = _inproj_attn_call(*ip_l, w_c, bm_lat, seq, True, False, rope_tabs, qk_gains)
        z_l, xbc_l, dt_l = _inproj_ssm_call(*ip_l, w_z, w_x, w_dt, bm_lat, seq, True)

        kv_a_c = (qk_a_c.reshape(batch, lc, QK_W), v_a_c)
        kv_c_c = (qk_c_c.reshape(batch, lc, QK_W), v_c_c.reshape(batch, lc, N_KV * LANES))
        kv_a = (qk_a.reshape(batch, seq, QK_W), v_a)
        kv_c = (qk_c.reshape(batch, seq, QK_W), v_c.reshape(batch, seq, N_KV * LANES))

        ya = _attn_call(kv_a[0], [kv_a_c], kv_a, a_sink[l], bq_lat, True).reshape(batch * seq, -1)
        yc = _attn_call(kv_c[0], [kv_c, kv_c_c], None, None, bq_lat, False).reshape(batch * seq, -1)

        ssm_args = (ssm_conv_w[l], ssm_conv_b[l], dtb, alog, d_skip, ssm_norm[l])
        yf_c, hf_c = _ssd_call(xbc_c, dt_c, None, None, zero_state, *ssm_args, batch, lc, False)
        ys_c, hb_c = _ssd_call(xbc_c, dt_c, z_c, yf_c, zero_state, *ssm_args, batch, lc, True)
        yf, _ = _ssd_call(xbc_l, dt_l, None, None, hf_c, *ssm_args, batch, seq, False)
        ys, _ = _ssd_call(xbc_l, dt_l, z_l, yf, hb_c, *ssm_args, batch, seq, True)

        wo = (w_oa[l].astype(BF16), w_ob[l].astype(BF16), w_oc[l].astype(BF16), w_out[l].astype(BF16))
        wf = (ffn_w_up[l].astype(BF16), ffn_w_gate[l].astype(BF16), ffn_conv_w[l], ffn_conv_b[l],
              ffn_w_down[l].astype(BF16))
        xl = _merge_call(xl, mod_lat, norm1[l], w_g, ya, ys, yc, *wo, bm_lat, seq, True)
        xl = _ffn_call(xl, mod_lat, norm2[l], *wf, final_norm if last else None, bm_lat, seq, True)

        if not last:
            ya_c = _attn_call(kv_a_c[0], [kv_a_c], None, a_sink[l], bq_ctx, True).reshape(batch * lc, -1)
            yc_c = _attn_call(kv_c_c[0], [kv_c_c], None, None, bq_ctx, False).reshape(batch * lc, -1)
            xc = _merge_call(xc, mod_ctx, norm1[l], w_g, ya_c, ys_c, yc_c, *wo, bm_ctx, lc, False)
            xc = _ffn_call(xc, mod_ctx, norm2[l], *wf, None, bm_ctx, lc, False)

    return xl.reshape(batch, seq, d)
```

```python
import functools
import math

import jax
import jax.numpy as jnp
from jax import lax
from jax.experimental import pallas as pl
from jax.experimental.pallas import tpu as pltpu

F32 = jnp.float32
BF16 = jnp.bfloat16

D_MODEL = 1024
GRID_W = 64
HEAD_DIM = 64
ROPE_BASE = 10000.0
EPS = 1e-6
WINDOW = 128
N_HEADS = 8
N_KV = 2
QKV_W = (N_HEADS + 2 * N_KV) * HEAD_DIM
QK_W = N_HEADS * HEAD_DIM + N_KV * 128
LOG2E = math.log2(math.e)
Q_SCALE = HEAD_DIM ** -0.5 * LOG2E
ATTN_KEY_TILE = 256
ATTN_SCORE_SLOTS = 3
SSM_HEADS = 16
SSM_P = 64
SSM_INNER = SSM_HEADS * SSM_P
SSM_GROUPS = 2
SSM_N = 128
SSM_CHUNK = 128
SSD_CHUNKS_PER_STEP = 8
SSM_XBC = SSM_INNER + 2 * SSM_GROUPS * SSM_N
D_FF = 2816

LANES = 128
SUBLANES = 8
VMEM_LIMIT_BYTES = 56 * 1024 * 1024

NEG_INF = float("-inf")


def _params(*sem):
    return pltpu.CompilerParams(dimension_semantics=sem, vmem_limit_bytes=VMEM_LIMIT_BYTES)


def _full(shape):
    n = len(shape)
    return pl.BlockSpec(shape, lambda *_: (0,) * n, pipeline_mode=pl.Buffered(1))


def _sigmoid(x):
    return 1.0 / (1.0 + jnp.exp(-x))


def _silu(x):
    return x * _sigmoid(x)


def _norm_mod(x, g, shift, scale):
    ms = jnp.mean(x * x, axis=-1, keepdims=True)
    return (x * lax.rsqrt(ms + EPS) * g) * (1.0 + scale) + shift


def _dot(a, b):
    return jnp.dot(a, b, preferred_element_type=F32)


def _dot_nt(a, b):
    return lax.dot_general(a, b, (((1,), (1,)), ((), ())), preferred_element_type=F32)


def _split3(x):
    hi = x.astype(BF16)
    r1 = x - hi.astype(F32)
    mid = r1.astype(BF16)
    lo = (r1 - mid.astype(F32)).astype(BF16)
    return hi, mid, lo


def _mod_kernel(c_ref, w_ref, b_ref, o_ref):
    s = _silu(c_ref[...])
    o_ref[...] = jnp.dot(s, w_ref[...], preferred_element_type=F32,
                         precision=lax.Precision.HIGHEST) + b_ref[...]


def _mod_call(c8, w, b):
    d, n = w.shape
    bn = 1536
    return pl.pallas_call(
        _mod_kernel,
        grid=(n // bn,),
        in_specs=[_full((SUBLANES, d)),
                  pl.BlockSpec((d, bn), lambda j: (0, j)),
                  pl.BlockSpec((1, bn), lambda j: (0, j))],
        out_specs=pl.BlockSpec((SUBLANES, bn), lambda j: (0, j)),
        out_shape=jax.ShapeDtypeStruct((SUBLANES, n), F32),
        compiler_params=_params("arbitrary"),
        name="mod_proj",
    )(c8, w, b.reshape(1, n))


def _mod_spec(bm, seq_len, per_seq):
    if per_seq:
        return pl.BlockSpec((None, 6, D_MODEL), lambda i: ((i * bm) // seq_len, 0, 0))
    return pl.BlockSpec((None, 6, D_MODEL), lambda i: (0, 0, 0))


def _rope(y, cos, sin_signed):
    lane = lax.broadcasted_iota(jnp.int32, y.shape, 1)
    swapped = jnp.where((lane & 1) == 0, pltpu.roll(y, LANES - 1, 1), pltpu.roll(y, 1, 1))
    return y * cos + swapped * sin_signed


def _qkv_epilogue(u, qk_ref, v_ref, rope_tabs, qk_norm, v_transposed):
    n_q = N_HEADS * HEAD_DIM // LANES
    lane = lax.broadcasted_iota(jnp.int32, (u.shape[0], LANES), 1)
    low = lane < HEAD_DIM
    for t in range(n_q + 2):
        y = u[:, t * LANES:(t + 1) * LANES]
        if t <= n_q:
            if qk_norm is not None:
                qg, kg, pool = qk_norm
                sq = y * y
                hi = sq.astype(BF16)
                lo = (sq - hi.astype(F32)).astype(BF16)
                ss = _dot(hi, pool) + _dot(lo, pool)
                y = y * lax.rsqrt(ss * (1.0 / HEAD_DIM) + EPS) * (qg if t < n_q else kg)
            if rope_tabs is not None:
                y = _rope(y, *rope_tabs)
        if t < n_q:
            qk_ref[:, t * LANES:(t + 1) * LANES] = (y * Q_SCALE).astype(BF16)
        elif t == n_q:
            qk_ref[:, t * LANES:(t + 1) * LANES] = jnp.where(low, y, 0.0).astype(BF16)
            qk_ref[:, (t + 1) * LANES:(t + 2) * LANES] = jnp.where(
                low, pltpu.roll(y, HEAD_DIM, 1), 0.0).astype(BF16)
        else:
            ones_col = jnp.where(lane == HEAD_DIM, 1.0, 0.0)
            v0 = jnp.where(low, y, ones_col)
            v1 = jnp.where(low, pltpu.roll(y, HEAD_DIM, 1), ones_col)
            if v_transposed:
                v_ref[0] = v0.T.astype(BF16)
                v_ref[1] = v1.T.astype(BF16)
            else:
                v_ref[:, 0:LANES] = v0.astype(BF16)
                v_ref[:, LANES:2 * LANES] = v1.astype(BF16)


def _inproj_kernel(*refs, rope):
    x_ref, mod_ref, g_ref, wa_ref, wc_ref, wz_ref, wx_ref, wdt_ref, qg_ref, kg_ref, pool_ref = refs[:11]
    k = 11
    rope_tabs = None
    if rope:
        rope_tabs = (refs[k][...], refs[k + 1][...])
        k += 2
    qka_ref, va_ref, qkc_ref, vc_ref, z_ref, xbc_ref, dt_ref = refs[k:k + 7]
    h = _norm_mod(x_ref[...], g_ref[...], mod_ref[0:1, :], mod_ref[1:2, :]).astype(BF16)
    _qkv_epilogue(_dot(h, wa_ref[...]), qka_ref, va_ref, rope_tabs, None, True)
    _qkv_epilogue(_dot(h, wc_ref[...]), qkc_ref, vc_ref, rope_tabs,
                  (qg_ref[...], kg_ref[...], pool_ref[...]), False)
    z_ref[...] = _dot(h, wz_ref[...])
    xbc_ref[...] = _dot(h, wx_ref[...])
    dt_ref[...] = _dot(h, wdt_ref[...])


def _inproj_call(x2d, mod, g, w_a, w_c, w_z, w_x, w_dt, qk_gains, bm, seq_len, per_seq, rope_tabs=None):
    t, d = x2d.shape
    rope = rope_tabs is not None
    qg, kg = qk_gains
    lane_head = jnp.arange(LANES) // HEAD_DIM
    pool = (lane_head[:, None] == lane_head[None, :]).astype(BF16)
    args = [x2d, mod, g.reshape(1, d), w_a, w_c, w_z, w_x, w_dt,
            jnp.tile(qg, LANES // HEAD_DIM).reshape(1, LANES),
            jnp.tile(kg, LANES // HEAD_DIM).reshape(1, LANES), pool]
    row = lambda n: pl.BlockSpec((bm, n), lambda i: (i, 0))
    specs = [row(d), _mod_spec(bm, seq_len, per_seq), _full((1, d)), _full(w_a.shape), _full(w_c.shape),
             _full(w_z.shape), _full(w_x.shape), _full(w_dt.shape),
             _full((1, LANES)), _full((1, LANES)), _full((LANES, LANES))]
    if rope:
        nblk = seq_len // bm
        args += list(rope_tabs)
        specs += [pl.BlockSpec((bm, LANES), lambda i: (i % nblk, 0))] * 2
    return pl.pallas_call(
        functools.partial(_inproj_kernel, rope=rope),
        grid=(t // bm,),
        in_specs=specs,
        out_specs=[row(QK_W), pl.BlockSpec((N_KV, LANES, bm), lambda i: (0, 0, i)),
                   row(QK_W), row(N_KV * LANES), row(SSM_INNER), row(SSM_XBC), row(LANES)],
        out_shape=[jax.ShapeDtypeStruct((t, QK_W), BF16), jax.ShapeDtypeStruct((N_KV, LANES, t), BF16),
                   jax.ShapeDtypeStruct((t, QK_W), BF16), jax.ShapeDtypeStruct((t, N_KV * LANES), BF16),
                   jax.ShapeDtypeStruct((t, SSM_INNER), F32), jax.ShapeDtypeStruct((t, SSM_XBC), F32),
                   jax.ShapeDtypeStruct((t, LANES), F32)],
        compiler_params=_params("parallel"),
        name="inproj",
    )(*args)


def _attn_t_kernel(*refs, bq, dense_lens, win_len, has_sink):
    q_ref = refs[0]
    k = 1
    segs = []
    for _ in dense_lens:
        segs.append((refs[k:k + 2], refs[k + 2:k + 4]))
        k += 4
    if win_len:
        win_k, win_v = refs[k:k + 2], refs[k + 2:k + 4]
        k += 4
    if has_sink:
        sink_ref = refs[k]
        k += 1
    o_ref, s_ref = refs[k:k + 2]
    i = pl.program_id(1)
    tk = ATTN_KEY_TILE

    tiles = [("dense", j, o) for j, lk in enumerate(dense_lens) for o in range(0, lk, tk)]
    if win_len:
        w = bq + 2 * WINDOW
        start = pl.multiple_of(jnp.clip(i * bq - WINDOW, 0, win_len - w), LANES)
        tiles += [("win", 0, o) for o in range(0, w, tk)]
        q_pos = i * bq + lax.broadcasted_iota(jnp.int32, (tk, bq), 1)
        k_iota = lax.broadcasted_iota(jnp.int32, (tk, bq), 0)

    def k_tile(g, t):
        kind, j, o = t
        if kind == "dense":
            return segs[j][0][g][o:o + tk, :]
        return win_k[g][pl.ds(pl.multiple_of(start + o, LANES), tk), :]

    def v_tile(g, t):
        kind, j, o = t
        if kind == "dense":
            return segs[j][1][g][:, o:o + tk]
        return win_v[g][:, pl.ds(pl.multiple_of(start + o, LANES), tk)]

    def head_q(h):
        tile = q_ref[:, (h // 2) * LANES:(h // 2 + 1) * LANES]
        return tile if h % 2 == 0 else pltpu.roll(tile.astype(F32), HEAD_DIM, 1).astype(BF16)

    def score_tile(h, qh, n, t, macc):
        s = _dot_nt(k_tile(h // hpk, t), qh)
        if t[0] == "win":
            s = jnp.where(jnp.abs(q_pos - (start + t[2] + k_iota)) <= WINDOW, s, NEG_INF)
        s_ref[h % 2, n * tk:(n + 1) * tk, :] = s
        mt = s.reshape(tk // SUBLANES, SUBLANES, bq).max(axis=0)
        return mt if macc is None else jnp.maximum(macc, mt)

    def finish_max(h, macc):
        m = macc.max(axis=0, keepdims=True)
        if has_sink:
            m = jnp.maximum(m, sink_ref[0, h] * LOG2E)
        return m

    hpk = N_HEADS // N_KV
    row = lax.broadcasted_iota(jnp.int32, (LANES, bq), 0)
    qh = head_q(0)
    macc = None
    for n, t in enumerate(tiles):
        macc = score_tile(0, qh, n, t, macc)
    m = finish_max(0, macc)
    o_even = None
    for h in range(N_HEADS):
        nxt = h + 1 < N_HEADS
        if nxt:
            qn = head_q(h + 1)
        macc = None
        acc = jnp.zeros((LANES, bq), F32)
        for n, t in enumerate(tiles):
            if nxt:
                macc = score_tile(h + 1, qn, n, t, macc)
            p = jnp.exp2(s_ref[h % 2, n * tk:(n + 1) * tk, :] - m).astype(BF16)
            acc = acc + _dot(v_tile(h // hpk, t), p)
        l = acc[HEAD_DIM:HEAD_DIM + 1, :]
        if has_sink:
            l = l + jnp.exp2(sink_ref[0, h] * LOG2E - m)
        o = acc / l
        if h % 2 == 0:
            o_even = o
        else:
            pair = jnp.where(row < HEAD_DIM, o_even, pltpu.roll(o, HEAD_DIM, 0))
            o_ref[:, (h // 2) * LANES:(h // 2 + 1) * LANES] = pair.T.astype(o_ref.dtype)
        if nxt:
            m = finish_max(h + 1, macc)


def _attn_rows_kernel(*refs, bq, dense_lens, has_sink):
    q_ref = refs[0]
    k = 1
    segs = []
    for lk in dense_lens:
        segs.append((refs[k:k + 2], refs[k + 2:k + 4], lk))
        k += 4
    if has_sink:
        sink_ref = refs[k]
        k += 1
    o_ref, s_ref = refs[k:k + 2]
    n_slots = s_ref.shape[0]
    lane = lax.broadcasted_iota(jnp.int32, (bq, LANES), 1)
    hpk = N_HEADS // N_KV
    o_even = None
    for h in range(N_HEADS):
        g = h // hpk
        slot = h % n_slots
        tile = q_ref[:, (h // 2) * LANES:(h // 2 + 1) * LANES]
        qh = tile if h % 2 == 0 else pltpu.roll(tile.astype(F32), HEAD_DIM, 1).astype(BF16)
        off = 0
        for k_refs, _, lk in segs:
            s_ref[slot, :, off:off + lk] = _dot_nt(qh, k_refs[g][...])
            off += lk
        s = s_ref[slot]
        m = s.max(axis=-1, keepdims=True)
        if has_sink:
            sink = sink_ref[0, h] * LOG2E
            m = jnp.maximum(m, sink)
        p = jnp.exp2(s - m).astype(BF16)
        acc = jnp.zeros((bq, LANES), F32)
        off = 0
        for _, v_refs, lk in segs:
            acc = acc + _dot(p[:, off:off + lk], v_refs[g][...])
            off += lk
        l = acc[:, HEAD_DIM:HEAD_DIM + 1]
        if has_sink:
            l = l + jnp.exp2(sink - m)
        o = acc / l
        if h % 2 == 0:
            o_even = o
        else:
            pair = jnp.where(lane < HEAD_DIM, o_even, pltpu.roll(o, HEAD_DIM, 1))
            o_ref[:, (h // 2) * LANES:(h // 2 + 1) * LANES] = pair.astype(o_ref.dtype)


def _attn_call(q_arr, dense_kv, win_kv, sink, bq, transposed):
    b, lq, _ = q_arr.shape
    q_w = N_HEADS * HEAD_DIM
    k_blk = q_w // LANES
    args = [q_arr]
    specs = [pl.BlockSpec((None, bq, q_w), lambda bi, i: (bi, i, 0))]

    def kv_specs(qk, v):
        lk = qk.shape[1]
        ks = [pl.BlockSpec((None, lk, LANES), functools.partial(lambda bi, i, j: (bi, 0, j), j=k_blk + j))
              for j in range(N_KV)]
        if transposed:
            vs = [pl.BlockSpec((None, LANES, lk), functools.partial(lambda bi, i, j: (j, 0, bi), j=j))
                  for j in range(N_KV)]
        else:
            vs = [pl.BlockSpec((None, lk, LANES), functools.partial(lambda bi, i, j: (bi, 0, j), j=j))
                  for j in range(N_KV)]
        return [qk] * N_KV + [v] * N_KV, ks + vs

    for qk, v in dense_kv:
        a, s = kv_specs(qk, v)
        args += a
        specs += s
    dense_lens = tuple(qk.shape[1] for qk, _ in dense_kv)
    n_keys = sum(dense_lens)
    win_len = 0
    if win_kv is not None:
        assert transposed
        win_len = win_kv[0].shape[1]
        assert win_len >= bq + 2 * WINDOW
        n_keys += bq + 2 * WINDOW
        a, s = kv_specs(*win_kv)
        args += a
        specs += s
    if sink is not None:
        args.append(sink.reshape(1, N_HEADS))
        specs.append(pl.BlockSpec(memory_space=pltpu.SMEM))
    if transposed:
        body = functools.partial(_attn_t_kernel, bq=bq, dense_lens=dense_lens, win_len=win_len,
                                 has_sink=sink is not None)
        scratch = pltpu.VMEM((2, n_keys, bq), F32)
    else:
        body = functools.partial(_attn_rows_kernel, bq=bq, dense_lens=dense_lens, has_sink=sink is not None)
        scratch = pltpu.VMEM((ATTN_SCORE_SLOTS, bq, n_keys), F32)
    return pl.pallas_call(
        body,
        grid=(b, lq // bq),
        in_specs=specs,
        out_specs=pl.BlockSpec((None, bq, q_w), lambda bi, i: (bi, i, 0)),
        out_shape=jax.ShapeDtypeStruct((b, lq, q_w), BF16),
        scratch_shapes=[scratch],
        compiler_params=_params("parallel", "parallel"),
        name="attn_t" if transposed else "attn_rows",
    )(*args)


def _pair_bcast(arr, c0, rows):
    lane = lax.broadcasted_iota(jnp.int32, (rows, LANES), 1)
    return jnp.where(lane < SSM_P, arr[:, c0:c0 + 1], arr[:, c0 + 1:c0 + 2])


def _ssd_kernel(*refs, rev, final, nb):
    (x_ref, xp_ref, xn_ref, dt_ref, cw_ref, cb_ref, dtb_ref, alog_ref, h0_ref) = refs[:9]
    k = 9
    if final:
        yo_ref, z_ref, dsk_ref, ng_ref = refs[k:k + 4]
        k += 4
    y_ref, hl_ref, h_ref = refs[k:k + 3]
    if final:
        gated_ref = refs[k + 3]

    c = pl.program_id(1)
    bidx = (nb - 1 - c) if rev else c
    q = SSM_CHUNK
    rows_blk = x_ref.shape[0]

    @pl.when(c == 0)
    def _():
        h_ref[...] = h0_ref[...]

    x = x_ref[...]
    prev = jnp.where(bidx > 0, xp_ref[SUBLANES - 1:SUBLANES, :], 0.0)
    nxt = jnp.where(bidx < nb - 1, xn_ref[0:1, :], 0.0)
    row = lax.broadcasted_iota(jnp.int32, x.shape, 0)
    xm1 = jnp.where(row == 0, prev, pltpu.roll(x, 1, 0))
    xp1 = jnp.where(row == rows_blk - 1, nxt, pltpu.roll(x, rows_blk - 1, 0))
    xbc_blk = _silu(xm1 * cw_ref[0:1, :] + x * cw_ref[1:2, :] + xp1 * cw_ref[2:3, :] + cb_ref[...])

    ri = lax.broadcasted_iota(jnp.int32, (q, q), 0)
    ci = lax.broadcasted_iota(jnp.int32, (q, q), 1)
    causal = (ci >= ri) if rev else (ci <= ri)
    tri = causal.astype(BF16)
    tri_t = ((ri >= ci) if rev else (ri <= ci)).astype(BF16)
    d_off = SSM_HEADS if rev else 0
    hp = SSM_HEADS // 2
    ppg = hp // SSM_GROUPS
    lane = lax.broadcasted_iota(jnp.int32, (q, LANES), 1)
    last = 0 if rev else q - 1

    n_sub = rows_blk // q
    for sub in (range(n_sub - 1, -1, -1) if rev else range(n_sub)):
        rs = slice(sub * q, (sub + 1) * q)
        xbc = xbc_blk[rs, :]
        dtr = dt_ref[rs, :] + dtb_ref[...]
        dtv = jnp.maximum(dtr, 0.0) + jnp.log(1.0 + jnp.exp(-jnp.abs(dtr)))
        dta = dtv * (-jnp.exp(alog_ref[...]))
        dta_t = dta.T
        acs = jnp.zeros((q, LANES), F32)
        acs_t = jnp.zeros((LANES, q), F32)
        for part, part_t in zip(_split3(dta), _split3(dta_t)):
            acs = acs + _dot(tri, part)
            acs_t = acs_t + _dot(part_t, tri_t)
        tot = acs[last:last + 1, :]

        for grp in range(SSM_GROUPS):
            b_g = xbc[:, SSM_INNER + grp * SSM_N:SSM_INNER + (grp + 1) * SSM_N]
            c_g = xbc[:, SSM_INNER + (SSM_GROUPS + grp) * SSM_N:SSM_INNER + (SSM_GROUPS + grp + 1) * SSM_N]
            b_bf = b_g.astype(BF16)
            c_bf = c_g.astype(BF16)
            bt_bf = b_g.T.astype(BF16)
            cb = _dot_nt(c_bf, b_bf)
            for pp in range(ppg):
                pr = grp * ppg + pp
                c0 = d_off + 2 * pr
                sl = slice(pr * LANES, (pr + 1) * LANES)
                xs_p = xbc[:, sl]
                xdt = xs_p * _pair_bcast(dtv, c0, q)
                xdt_bf = xdt.astype(BF16)
                acs_p = _pair_bcast(acs, c0, q)
                tot_p = _pair_bcast(tot, c0, 1)
                y_heads = []
                for hh in range(2):
                    col = c0 + hh
                    seg = acs[:, col:col + 1] - acs_t[col:col + 1, :]
                    lmat = jnp.exp(jnp.where(causal, seg, NEG_INF))
                    y_heads.append(_dot((cb * lmat).astype(BF16), xdt_bf))
                y_diag = jnp.where(lane < SSM_P, y_heads[0], y_heads[1])
                hs = slice(pp * LANES, (pp + 1) * LANES)
                h_p = h_ref[grp, :, hs]
                y_p = y_diag + _dot(c_bf, h_p.astype(BF16)) * jnp.exp(acs_p)
                upd = _dot(bt_bf, (xdt * jnp.exp(tot_p - acs_p)).astype(BF16))
                h_ref[grp, :, hs] = h_p * jnp.exp(tot_p) + upd
                if final:
                    y_all = y_p + yo_ref[rs, sl] + xs_p * dsk_ref[:, sl]
                    gated_ref[rs, sl] = y_all * _silu(z_ref[rs, sl])
                else:
                    y_ref[rs, sl] = y_p
    if final:
        gt = gated_ref[...]
        ms = jnp.mean(gt * gt, axis=-1, keepdims=True)
        y_ref[...] = (gt * lax.rsqrt(ms + EPS) * ng_ref[...]).astype(y_ref.dtype)

    @pl.when(c == nb - 1)
    def _():
        hl_ref[...] = h_ref[...]


def _ssd_call(xbc, dt, z, y_other, h0, conv_w, conv_b, dt_bias128, alog128, d_skip, norm_g,
              batch, seq_len, rev):
    final = y_other is not None
    rows_blk = min(SSD_CHUNKS_PER_STEP * SSM_CHUNK, seq_len)
    nb = seq_len // rows_blk
    hb = rows_blk // SUBLANES
    nrow8 = batch * seq_len // SUBLANES

    def bidx(c):
        return (nb - 1 - c) if rev else c

    def main(n):
        return pl.BlockSpec((rows_blk, n), lambda b, c: (b * nb + bidx(c), 0))

    prev_spec = pl.BlockSpec((SUBLANES, SSM_XBC),
                             lambda b, c: (jnp.maximum((b * nb + bidx(c)) * hb - 1, 0), 0))
    next_spec = pl.BlockSpec((SUBLANES, SSM_XBC),
                             lambda b, c: (jnp.minimum((b * nb + bidx(c) + 1) * hb, nrow8 - 1), 0))
    state_spec = pl.BlockSpec((None, SSM_GROUPS, SSM_N, SSM_INNER // SSM_GROUPS),
                              lambda b, c: (b, 0, 0, 0))
    args = [xbc, xbc, xbc, dt, conv_w, conv_b.reshape(1, SSM_XBC), dt_bias128, alog128, h0]
    specs = [main(SSM_XBC), prev_spec, next_spec, main(LANES), _full(conv_w.shape),
             _full((1, SSM_XBC)), _full((1, LANES)), _full((1, LANES)), state_spec]
    scratch = [pltpu.VMEM((SSM_GROUPS, SSM_N, SSM_INNER // SSM_GROUPS), F32)]
    if final:
        args += [y_other, z, d_skip, norm_g.reshape(1, SSM_INNER)]
        specs += [main(SSM_INNER), main(SSM_INNER), _full((1, SSM_INNER)), _full((1, SSM_INNER))]
        scratch.append(pltpu.VMEM((rows_blk, SSM_INNER), F32))
    return pl.pallas_call(
        functools.partial(_ssd_kernel, rev=rev, final=final, nb=nb),
        grid=(batch, nb),
        in_specs=specs,
        out_specs=[main(SSM_INNER), state_spec],
        out_shape=[jax.ShapeDtypeStruct((batch * seq_len, SSM_INNER), BF16 if final else F32),
                   jax.ShapeDtypeStruct(h0.shape, F32)],
        scratch_shapes=scratch,
        compiler_params=_params("parallel", "arbitrary"),
        name="ssd_bwd_final" if final else "ssd_fwd",
    )(*args)


def _merge_kernel(x_ref, mod_ref, g_ref, wg_ref, ya_ref, ys_ref, yc_ref, woa_ref, wob_ref, woc_ref,
                  wout_ref, o_ref):
    x = x_ref[...]
    d = x.shape[-1]
    h = _norm_mod(x, g_ref[...], mod_ref[0:1, :], mod_ref[1:2, :]).astype(BF16)
    m = (_sigmoid(_dot(h, wg_ref[:, 0:d])) * _dot(ya_ref[...], woa_ref[...])
         + _sigmoid(_dot(h, wg_ref[:, d:2 * d])) * _dot(ys_ref[...], wob_ref[...])
         + _sigmoid(_dot(h, wg_ref[:, 2 * d:3 * d])) * _dot(yc_ref[...], woc_ref[...]))
    o_ref[...] = x + mod_ref[2:3, :] * _dot(m.astype(BF16), wout_ref[...])


def _merge_call(x2d, mod, g, wg, ya, ys, yc, woa, wob, woc, wout, bm, seq_len, per_seq):
    t, d = x2d.shape
    row = lambda n: pl.BlockSpec((bm, n), lambda i: (i, 0))
    return pl.pallas_call(
        _merge_kernel,
        grid=(t // bm,),
        in_specs=[row(d), _mod_spec(bm, seq_len, per_seq), _full((1, d)), _full(wg.shape),
                  row(ya.shape[1]), row(ys.shape[1]), row(yc.shape[1]),
                  _full(woa.shape), _full(wob.shape), _full(woc.shape), _full(wout.shape)],
        out_specs=row(d),
        out_shape=jax.ShapeDtypeStruct((t, d), F32),
        compiler_params=_params("parallel"),
        name="merge",
    )(x2d, mod, g.reshape(1, d), wg, ya, ys, yc, woa, wob, woc, wout)


def _ffn_kernel(*refs, bm, seq_len, final_norm, ff_chunk):
    x_ref, xp_ref, xn_ref, mod_ref, g_ref, wup_ref, wgt_ref, cw_ref, cb_ref, wdn_ref = refs[:10]
    k = 10
    if final_norm:
        fg_ref = refs[k]
        k += 1
    o_ref = refs[k]
    i = pl.program_id(0)
    x = x_ref[...]
    g = g_ref[...]
    shift, scale = mod_ref[3:4, :], mod_ref[4:5, :]
    hm = _norm_mod(x, g, shift, scale)
    h = hm.astype(BF16)
    h_ext = jnp.concatenate([_norm_mod(xp_ref[...], g, shift, scale), hm,
                             _norm_mod(xn_ref[...], g, shift, scale)], axis=0).astype(BF16)
    pos = (i * bm + lax.broadcasted_iota(jnp.int32, (bm, 1), 0)) % seq_len
    has_prev = pos > 0
    has_next = pos < seq_len - 1
    n_ext = bm + 2 * SUBLANES
    acc = jnp.zeros(x.shape, F32)
    for c0 in range(0, D_FF, ff_chunk):
        cs = slice(c0, c0 + ff_chunk)
        gt = _dot(h_ext, wgt_ref[:, cs])
        gm1 = pltpu.roll(gt, 1, 0)[SUBLANES:SUBLANES + bm, :]
        gp1 = pltpu.roll(gt, n_ext - 1, 0)[SUBLANES:SUBLANES + bm, :]
        conv = (jnp.where(has_prev, gm1, 0.0) * cw_ref[0:1, cs]
                + gt[SUBLANES:SUBLANES + bm, :] * cw_ref[1:2, cs]
                + jnp.where(has_next, gp1, 0.0) * cw_ref[2:3, cs] + cb_ref[:, cs])
        act = (_silu(conv) * _dot(h, wup_ref[:, cs])).astype(BF16)
        acc = acc + _dot(act, wdn_ref[cs, :])
    out = x + mod_ref[5:6, :] * acc
    if final_norm:
        ms = jnp.mean(out * out, axis=-1, keepdims=True)
        out = out * lax.rsqrt(ms + EPS) * fg_ref[...]
    o_ref[...] = out


def _ffn_call(x2d, mod, g, wup, wgt, conv_w, conv_b, wdn, final_g, bm, seq_len, per_seq):
    t, d = x2d.shape
    hb = bm // SUBLANES
    nrow8 = t // SUBLANES
    args = [x2d, x2d, x2d, mod, g.reshape(1, d), wup, wgt, conv_w, conv_b.reshape(1, D_FF), wdn]
    specs = [pl.BlockSpec((bm, d), lambda i: (i, 0)),
             pl.BlockSpec((SUBLANES, d), lambda i: (jnp.maximum(i * hb - 1, 0), 0)),
             pl.BlockSpec((SUBLANES, d), lambda i: (jnp.minimum((i + 1) * hb, nrow8 - 1), 0)),
             _mod_spec(bm, seq_len, per_seq), _full((1, d)), _full(wup.shape), _full(wgt.shape),
             _full(conv_w.shape), _full((1, D_FF)), _full(wdn.shape)]
    if final_g is not None:
        args.append(final_g.reshape(1, d))
        specs.append(_full((1, d)))
    return pl.pallas_call(
        functools.partial(_ffn_kernel, bm=bm, seq_len=seq_len, final_norm=final_g is not None,
                          ff_chunk=D_FF),
        grid=(t // bm,),
        in_specs=specs,
        out_specs=pl.BlockSpec((bm, d), lambda i: (i, 0)),
        out_shape=jax.ShapeDtypeStruct((t, d), F32),
        compiler_params=_params("parallel"),
        name="ffn",
    )(*args)


def _rope_tables(seq_len):
    rows = seq_len // GRID_W
    t_row = jnp.repeat(jnp.arange(rows), GRID_W).astype(F32)
    t_col = jnp.tile(jnp.arange(GRID_W), rows).astype(F32)
    n = HEAD_DIM // 4
    inv = ROPE_BASE ** (-jnp.arange(n, dtype=F32) / n)
    ang = jnp.concatenate([t_row[:, None] * inv, t_col[:, None] * inv], axis=-1)
    cos = jnp.repeat(jnp.cos(ang), 2, axis=-1)
    sin = jnp.repeat(jnp.sin(ang), 2, axis=-1) * jnp.tile(jnp.array([-1.0, 1.0], F32), HEAD_DIM // 2)
    reps = LANES // HEAD_DIM
    return jnp.tile(cos, (1, reps)), jnp.tile(sin, (1, reps))


def _pad_lanes(v):
    return jnp.pad(v.reshape(1, -1), ((0, 0), (0, LANES - v.size)))


def kernel(x, c, ctx, c_ctx, w_mod, b_mod, norm1, norm2, w_in, a_sink, ssm_conv_w, ssm_conv_b,
           ssm_A_log, ssm_dt_bias, ssm_D, ssm_norm, c_q_norm, c_k_norm, w_oa, w_ob, w_oc, w_out,
           ffn_w_up, ffn_w_gate, ffn_conv_w, ffn_conv_b, ffn_w_down, final_norm):
    batch, seq, d = x.shape
    lc = ctx.shape[1]
    depth = w_mod.shape[0]
    bm_lat = min(512, seq)
    bm_ctx = min(512, batch * lc)
    bq_lat = min(256, seq)
    bq_ctx = min(256, lc)

    rope_tabs = _rope_tables(seq)
    c8 = jnp.zeros((SUBLANES, d), F32).at[:batch].set(c).at[batch].set(c_ctx)

    o_aq = 0
    o_bz = o_aq + QKV_W
    o_bx = o_bz + SSM_INNER
    o_bdt = o_bx + SSM_XBC
    o_cq = o_bdt + 2 * SSM_HEADS
    o_g = o_cq + QKV_W

    xl = x.reshape(batch * seq, d)
    xc = ctx.reshape(batch * lc, d)
    zero_state = jnp.zeros((batch, SSM_GROUPS, SSM_N, SSM_INNER // SSM_GROUPS), F32)

    for l in range(depth):
        last = l == depth - 1
        mod = _mod_call(c8, w_mod[l], b_mod[l])
        mod_lat = mod[:batch].reshape(batch, 6, d)
        mod_ctx = mod[batch:batch + 1].reshape(1, 6, d)

        wl = w_in[l]
        w_a = wl[:, o_aq:o_bz].astype(BF16)
        w_z = wl[:, o_bz:o_bx].astype(BF16)
        w_x = wl[:, o_bx:o_bdt].astype(BF16)
        w_dt = jnp.pad(wl[:, o_bdt:o_cq], ((0, 0), (0, LANES - 2 * SSM_HEADS))).astype(BF16)
        w_c = wl[:, o_cq:o_g].astype(BF16)
        w_g = wl[:, o_g:].astype(BF16)
        qk_gains = (c_q_norm[l], c_k_norm[l])
        dtb = _pad_lanes(ssm_dt_bias[l])
        alog = _pad_lanes(ssm_A_log[l])
        d_skip = jnp.repeat(ssm_D[l], SSM_P).reshape(1, SSM_INNER)

        w_all = (w_a, w_c, w_z, w_x, w_dt, qk_gains)
        qk_a_c, v_a_c, qk_c_c, v_c_c, z_c, xbc_c, dt_c = _inproj_call(
            xc, mod_ctx, norm1[l], *w_all, bm_ctx, lc, False)
        qk_a, v_a, qk_c, v_c, z_l, xbc_l, dt_l = _inproj_call(
            xl, mod_lat, norm1[l], *w_all, bm_lat, seq, True, rope_tabs)

        kv_a_c = (qk_a_c.reshape(batch, lc, QK_W), v_a_c)
        kv_c_c = (qk_c_c.reshape(batch, lc, QK_W), v_c_c.reshape(batch, lc, N_KV * LANES))
        kv_a = (qk_a.reshape(batch, seq, QK_W), v_a)
        kv_c = (qk_c.reshape(batch, seq, QK_W), v_c.reshape(batch, seq, N_KV * LANES))

        ya = _attn_call(kv_a[0], [kv_a_c], kv_a, a_sink[l], bq_lat, True).reshape(batch * seq, -1)
        yc = _attn_call(kv_c[0], [kv_c, kv_c_c], None, None, bq_lat, False).reshape(batch * seq, -1)

        ssm_args = (ssm_conv_w[l], ssm_conv_b[l], dtb, alog, d_skip, ssm_norm[l])
        yf_c, hf_c = _ssd_call(xbc_c, dt_c, None, None, zero_state, *ssm_args, batch, lc, False)
        ys_c, hb_c = _ssd_call(xbc_c, dt_c, z_c, yf_c, zero_state, *ssm_args, batch, lc, True)
        yf, _ = _ssd_call(xbc_l, dt_l, None, None, hf_c, *ssm_args, batch, seq, False)
        ys, _ = _ssd_call(xbc_l, dt_l, z_l, yf, hb_c, *ssm_args, batch, seq, True)

        wo = (w_oa[l].astype(BF16), w_ob[l].astype(BF16), w_oc[l].astype(BF16), w_out[l].astype(BF16))
        wf = (ffn_w_up[l].astype(BF16), ffn_w_gate[l].astype(BF16), ffn_conv_w[l], ffn_conv_b[l],
              ffn_w_down[l].astype(BF16))
        xl = _merge_call(xl, mod_lat, norm1[l], w_g, ya, ys, yc, *wo, bm_lat, seq, True)
        xl = _ffn_call(xl, mod_lat, norm2[l], *wf, final_norm if last else None, bm_lat, seq, True)

        if not last:
            ya_c = _attn_call(kv_a_c[0], [kv_a_c], None, a_sink[l], bq_ctx, True).reshape(batch * lc, -1)
            yc_c = _attn_call(kv_c_c[0], [kv_c_c], None, None, bq_ctx, False).reshape(batch * lc, -1)
            xc = _merge_call(xc, mod_ctx, norm1[l], w_g, ya_c, ys_c, yc_c, *wo, bm_ctx, lc, False)
            xc = _ffn_call(xc, mod_ctx, norm2[l], *wf, None, bm_ctx, lc, False)

    return xl.reshape(batch, seq, d)
```

```python
import functools
import math

import jax
import jax.numpy as jnp
from jax import lax
from jax.experimental import pallas as pl
from jax.experimental.pallas import tpu as pltpu

F32 = jnp.float32
BF16 = jnp.bfloat16

D_MODEL = 1024
GRID_W = 64
HEAD_DIM = 64
ROPE_BASE = 10000.0
EPS = 1e-6
WINDOW = 128
N_HEADS = 8
N_KV = 2
QKV_W = (N_HEADS + 2 * N_KV) * HEAD_DIM
QK_W = N_HEADS * HEAD_DIM + N_KV * 128
LOG2E = math.log2(math.e)
Q_SCALE = HEAD_DIM ** -0.5 * LOG2E
ATTN_KEY_TILE = 256
ATTN_SCORE_SLOTS = 3
SSM_HEADS = 16
SSM_P = 64
SSM_INNER = SSM_HEADS * SSM_P
SSM_GROUPS = 2
SSM_N = 128
SSM_CHUNK = 128
SSD_CHUNKS_PER_STEP = 8
SSM_XBC = SSM_INNER + 2 * SSM_GROUPS * SSM_N
D_FF = 2816

LANES = 128
SUBLANES = 8
VMEM_LIMIT_BYTES = 56 * 1024 * 1024

NEG_INF = float("-inf")


def _params(*sem):
    return pltpu.CompilerParams(dimension_semantics=sem, vmem_limit_bytes=VMEM_LIMIT_BYTES)


def _full(shape):
    n = len(shape)
    return pl.BlockSpec(shape, lambda *_: (0,) * n, pipeline_mode=pl.Buffered(1))


def _sigmoid(x):
    return 1.0 / (1.0 + jnp.exp(-x))


def _silu(x):
    return x * _sigmoid(x)


def _norm_mod(x, g, shift, scale):
    ms = jnp.mean(x * x, axis=-1, keepdims=True)
    return (x * lax.rsqrt(ms + EPS) * g) * (1.0 + scale) + shift


def _dot(a, b):
    return jnp.dot(a, b, preferred_element_type=F32)


def _dot_nt(a, b):
    return lax.dot_general(a, b, (((1,), (1,)), ((), ())), preferred_element_type=F32)


def _split3(x):
    hi = x.astype(BF16)
    r1 = x - hi.astype(F32)
    mid = r1.astype(BF16)
    lo = (r1 - mid.astype(F32)).astype(BF16)
    return hi, mid, lo


def _mod_kernel(c_ref, w_ref, b_ref, o_ref):
    s = _silu(c_ref[...])
    o_ref[...] = jnp.dot(s, w_ref[...], preferred_element_type=F32,
                         precision=lax.Precision.HIGHEST) + b_ref[...]


def _mod_call(c8, w, b):
    d, n = w.shape
    bn = 1536
    return pl.pallas_call(
        _mod_kernel,
        grid=(n // bn,),
        in_specs=[_full((SUBLANES, d)),
                  pl.BlockSpec((d, bn), lambda j: (0, j)),
                  pl.BlockSpec((1, bn), lambda j: (0, j))],
        out_specs=pl.BlockSpec((SUBLANES, bn), lambda j: (0, j)),
        out_shape=jax.ShapeDtypeStruct((SUBLANES, n), F32),
        compiler_params=_params("arbitrary"),
        name="mod_proj",
    )(c8, w, b.reshape(1, n))


def _mod_spec(bm, seq_len, per_seq):
    if per_seq:
        return pl.BlockSpec((None, 6, D_MODEL), lambda i: ((i * bm) // seq_len, 0, 0))
    return pl.BlockSpec((None, 6, D_MODEL), lambda i: (0, 0, 0))


def _rope(y, cos, sin_signed):
    lane = lax.broadcasted_iota(jnp.int32, y.shape, 1)
    swapped = jnp.where((lane & 1) == 0, pltpu.roll(y, LANES - 1, 1), pltpu.roll(y, 1, 1))
    return y * cos + swapped * sin_signed


def _qkv_epilogue(u, qk_ref, v_ref, rope_tabs, qk_norm, v_transposed):
    n_q = N_HEADS * HEAD_DIM // LANES
    lane = lax.broadcasted_iota(jnp.int32, (u.shape[0], LANES), 1)
    low = lane < HEAD_DIM
    for t in range(n_q + 2):
        y = u[:, t * LANES:(t + 1) * LANES]
        if t <= n_q:
            if qk_norm is not None:
                qg, kg, pool = qk_norm
                sq = y * y
                hi = sq.astype(BF16)
                lo = (sq - hi.astype(F32)).astype(BF16)
                ss = _dot(hi, pool) + _dot(lo, pool)
                y = y * lax.rsqrt(ss * (1.0 / HEAD_DIM) + EPS) * (qg if t < n_q else kg)
            if rope_tabs is not None:
                y = _rope(y, *rope_tabs)
        if t < n_q:
            qk_ref[:, t * LANES:(t + 1) * LANES] = (y * Q_SCALE).astype(BF16)
        elif t == n_q:
            qk_ref[:, t * LANES:(t + 1) * LANES] = jnp.where(low, y, 0.0).astype(BF16)
            qk_ref[:, (t + 1) * LANES:(t + 2) * LANES] = jnp.where(
                low, pltpu.roll(y, HEAD_DIM, 1), 0.0).astype(BF16)
        else:
            ones_col = jnp.where(lane == HEAD_DIM, 1.0, 0.0)
            v0 = jnp.where(low, y, ones_col)
            v1 = jnp.where(low, pltpu.roll(y, HEAD_DIM, 1), ones_col)
            if v_transposed:
                v_ref[0] = v0.T.astype(BF16)
                v_ref[1] = v1.T.astype(BF16)
            else:
                v_ref[:, 0:LANES] = v0.astype(BF16)
                v_ref[:, LANES:2 * LANES] = v1.astype(BF16)


def _inproj_kernel(*refs, rope, bm, seq_len):
    (x_ref, xp_ref, xn_ref, mod_ref, g_ref, wa_ref, wc_ref, wz_ref, wx_ref, wdt_ref, dtb_ref, cw_ref,
     cb_ref, qg_ref, kg_ref, pool_ref) = refs[:16]
    k = 16
    rope_tabs = None
    if rope:
        rope_tabs = (refs[k][...], refs[k + 1][...])
        k += 2
    qka_ref, va_ref, qkc_ref, vc_ref, z_ref, xbc_ref, dt_ref = refs[k:k + 7]
    i = pl.program_id(0)
    g = g_ref[...]
    shift, scale = mod_ref[0:1, :], mod_ref[1:2, :]
    hm = _norm_mod(x_ref[...], g, shift, scale)
    h = hm.astype(BF16)
    h_ext = jnp.concatenate([_norm_mod(xp_ref[...], g, shift, scale), hm,
                             _norm_mod(xn_ref[...], g, shift, scale)], axis=0).astype(BF16)
    _qkv_epilogue(_dot(h, wa_ref[...]), qka_ref, va_ref, rope_tabs, None, True)
    _qkv_epilogue(_dot(h, wc_ref[...]), qkc_ref, vc_ref, rope_tabs,
                  (qg_ref[...], kg_ref[...], pool_ref[...]), False)
    z_ref[...] = _silu(_dot(h, wz_ref[...]))
    dtr = _dot(h, wdt_ref[...]) + dtb_ref[...]
    dt_ref[...] = jnp.maximum(dtr, 0.0) + jnp.log(1.0 + jnp.exp(-jnp.abs(dtr)))
    xe = _dot(h_ext, wx_ref[...])
    n_ext = bm + 2 * SUBLANES
    pos = (i * bm + lax.broadcasted_iota(jnp.int32, (bm, 1), 0)) % seq_len
    xm1 = jnp.where(pos > 0, pltpu.roll(xe, 1, 0)[SUBLANES:SUBLANES + bm, :], 0.0)
    xp1 = jnp.where(pos < seq_len - 1, pltpu.roll(xe, n_ext - 1, 0)[SUBLANES:SUBLANES + bm, :], 0.0)
    xbc_ref[...] = _silu(xm1 * cw_ref[0:1, :] + xe[SUBLANES:SUBLANES + bm, :] * cw_ref[1:2, :]
                         + xp1 * cw_ref[2:3, :] + cb_ref[...])


def _inproj_call(x2d, mod, g, w_a, w_c, w_z, w_x, w_dt, dt_bias128, conv_w, conv_b, qk_gains, bm, seq_len,
                 per_seq, rope_tabs=None):
    t, d = x2d.shape
    rope = rope_tabs is not None
    qg, kg = qk_gains
    lane_head = jnp.arange(LANES) // HEAD_DIM
    pool = (lane_head[:, None] == lane_head[None, :]).astype(BF16)
    hb = bm // SUBLANES
    nrow8 = t // SUBLANES
    args = [x2d, x2d, x2d, mod, g.reshape(1, d), w_a, w_c, w_z, w_x, w_dt, dt_bias128, conv_w,
            conv_b.reshape(1, SSM_XBC),
            jnp.tile(qg, LANES // HEAD_DIM).reshape(1, LANES),
            jnp.tile(kg, LANES // HEAD_DIM).reshape(1, LANES), pool]
    row = lambda n: pl.BlockSpec((bm, n), lambda i: (i, 0))
    specs = [row(d),
             pl.BlockSpec((SUBLANES, d), lambda i: (jnp.maximum(i * hb - 1, 0), 0)),
             pl.BlockSpec((SUBLANES, d), lambda i: (jnp.minimum((i + 1) * hb, nrow8 - 1), 0)),
             _mod_spec(bm, seq_len, per_seq), _full((1, d)), _full(w_a.shape), _full(w_c.shape),
             _full(w_z.shape), _full(w_x.shape), _full(w_dt.shape), _full((1, LANES)),
             _full(conv_w.shape), _full((1, SSM_XBC)),
             _full((1, LANES)), _full((1, LANES)), _full((LANES, LANES))]
    if rope:
        nblk = seq_len // bm
        args += list(rope_tabs)
        specs += [pl.BlockSpec((bm, LANES), lambda i: (i % nblk, 0))] * 2
    return pl.pallas_call(
        functools.partial(_inproj_kernel, rope=rope, bm=bm, seq_len=seq_len),
        grid=(t // bm,),
        in_specs=specs,
        out_specs=[row(QK_W), pl.BlockSpec((N_KV, LANES, bm), lambda i: (0, 0, i)),
                   row(QK_W), row(N_KV * LANES), row(SSM_INNER), row(SSM_XBC), row(LANES)],
        out_shape=[jax.ShapeDtypeStruct((t, QK_W), BF16), jax.ShapeDtypeStruct((N_KV, LANES, t), BF16),
                   jax.ShapeDtypeStruct((t, QK_W), BF16), jax.ShapeDtypeStruct((t, N_KV * LANES), BF16),
                   jax.ShapeDtypeStruct((t, SSM_INNER), F32), jax.ShapeDtypeStruct((t, SSM_XBC), F32),
                   jax.ShapeDtypeStruct((t, LANES), F32)],
        compiler_params=_params("parallel"),
        name="inproj",
    )(*args)


def _attn_t_kernel(*refs, bq, dense_lens, win_len, has_sink):
    q_ref = refs[0]
    k = 1
    segs = []
    for _ in dense_lens:
        segs.append((refs[k:k + 2], refs[k + 2:k + 4]))
        k += 4
    if win_len:
        win_k, win_v = refs[k:k + 2], refs[k + 2:k + 4]
        k += 4
    if has_sink:
        sink_ref = refs[k]
        k += 1
    o_ref, s_ref = refs[k:k + 2]
    i = pl.program_id(1)
    tk = ATTN_KEY_TILE

    tiles = [("dense", j, o) for j, lk in enumerate(dense_lens) for o in range(0, lk, tk)]
    if win_len:
        w = bq + 2 * WINDOW
        start = pl.multiple_of(jnp.clip(i * bq - WINDOW, 0, win_len - w), LANES)
        tiles += [("win", 0, o) for o in range(0, w, tk)]
        q_pos = i * bq + lax.broadcasted_iota(jnp.int32, (tk, bq), 1)
        k_iota = lax.broadcasted_iota(jnp.int32, (tk, bq), 0)

    def k_tile(g, t):
        kind, j, o = t
        if kind == "dense":
            return segs[j][0][g][o:o + tk, :]
        return win_k[g][pl.ds(pl.multiple_of(start + o, LANES), tk), :]

    def v_tile(g, t):
        kind, j, o = t
        if kind == "dense":
            return segs[j][1][g][:, o:o + tk]
        return win_v[g][:, pl.ds(pl.multiple_of(start + o, LANES), tk)]

    def head_q(h):
        tile = q_ref[:, (h // 2) * LANES:(h // 2 + 1) * LANES]
        return tile if h % 2 == 0 else pltpu.roll(tile.astype(F32), HEAD_DIM, 1).astype(BF16)

    def score_tile(h, qh, n, t, macc):
        s = _dot_nt(k_tile(h // hpk, t), qh)
        if t[0] == "win":
            s = jnp.where(jnp.abs(q_pos - (start + t[2] + k_iota)) <= WINDOW, s, NEG_INF)
        s_ref[h % 2, n * tk:(n + 1) * tk, :] = s
        mt = s.reshape(tk // SUBLANES, SUBLANES, bq).max(axis=0)
        return mt if macc is None else jnp.maximum(macc, mt)

    def finish_max(h, macc):
        m = macc.max(axis=0, keepdims=True)
        if has_sink:
            m = jnp.maximum(m, sink_ref[0, h] * LOG2E)
        return m

    hpk = N_HEADS // N_KV
    row = lax.broadcasted_iota(jnp.int32, (LANES, bq), 0)
    qh = head_q(0)
    macc = None
    for n, t in enumerate(tiles):
        macc = score_tile(0, qh, n, t, macc)
    m = finish_max(0, macc)
    o_even = None
    for h in range(N_HEADS):
        nxt = h + 1 < N_HEADS
        if nxt:
            qn = head_q(h + 1)
        macc = None
        acc = jnp.zeros((LANES, bq), F32)
        for n, t in enumerate(tiles):
            if nxt:
                macc = score_tile(h + 1, qn, n, t, macc)
            p = jnp.exp2(s_ref[h % 2, n * tk:(n + 1) * tk, :] - m).astype(BF16)
            acc = acc + _dot(v_tile(h // hpk, t), p)
        l = acc[HEAD_DIM:HEAD_DIM + 1, :]
        if has_sink:
            l = l + jnp.exp2(sink_ref[0, h] * LOG2E - m)
        o = acc / l
        if h % 2 == 0:
            o_even = o
        else:
            pair = jnp.where(row < HEAD_DIM, o_even, pltpu.roll(o, HEAD_DIM, 0))
            o_ref[:, (h // 2) * LANES:(h // 2 + 1) * LANES] = pair.T.astype(o_ref.dtype)
        if nxt:
            m = finish_max(h + 1, macc)


def _attn_rows_kernel(*refs, bq, dense_lens, has_sink):
    q_ref = refs[0]
    k = 1
    segs = []
    for lk in dense_lens:
        segs.append((refs[k:k + 2], refs[k + 2:k + 4], lk))
        k += 4
    if has_sink:
        sink_ref = refs[k]
        k += 1
    o_ref, s_ref = refs[k:k + 2]
    n_slots = s_ref.shape[0]
    lane = lax.broadcasted_iota(jnp.int32, (bq, LANES), 1)
    hpk = N_HEADS // N_KV
    o_even = None
    for h in range(N_HEADS):
        g = h // hpk
        slot = h % n_slots
        tile = q_ref[:, (h // 2) * LANES:(h // 2 + 1) * LANES]
        qh = tile if h % 2 == 0 else pltpu.roll(tile.astype(F32), HEAD_DIM, 1).astype(BF16)
        off = 0
        for k_refs, _, lk in segs:
            s_ref[slot, :, off:off + lk] = _dot_nt(qh, k_refs[g][...])
            off += lk
        s = s_ref[slot]
        m = s.max(axis=-1, keepdims=True)
        if has_sink:
            sink = sink_ref[0, h] * LOG2E
            m = jnp.maximum(m, sink)
        p = jnp.exp2(s - m).astype(BF16)
        acc = jnp.zeros((bq, LANES), F32)
        off = 0
        for _, v_refs, lk in segs:
            acc = acc + _dot(p[:, off:off + lk], v_refs[g][...])
            off += lk
        l = acc[:, HEAD_DIM:HEAD_DIM + 1]
        if has_sink:
            l = l + jnp.exp2(sink - m)
        o = acc / l
        if h % 2 == 0:
            o_even = o
        else:
            pair = jnp.where(lane < HEAD_DIM, o_even, pltpu.roll(o, HEAD_DIM, 1))
            o_ref[:, (h // 2) * LANES:(h // 2 + 1) * LANES] = pair.astype(o_ref.dtype)


def _attn_call(q_arr, dense_kv, win_kv, sink, bq, transposed):
    b, lq, _ = q_arr.shape
    q_w = N_HEADS * HEAD_DIM
    k_blk = q_w // LANES
    args = [q_arr]
    specs = [pl.BlockSpec((None, bq, q_w), lambda bi, i: (bi, i, 0))]

    def kv_specs(qk, v):
        lk = qk.shape[1]
        ks = [pl.BlockSpec((None, lk, LANES), functools.partial(lambda bi, i, j: (bi, 0, j), j=k_blk + j))
              for j in range(N_KV)]
        if transposed:
            vs = [pl.BlockSpec((None, LANES, lk), functools.partial(lambda bi, i, j: (j, 0, bi), j=j))
                  for j in range(N_KV)]
        else:
            vs = [pl.BlockSpec((None, lk, LANES), functools.partial(lambda bi, i, j: (bi, 0, j), j=j))
                  for j in range(N_KV)]
        return [qk] * N_KV + [v] * N_KV, ks + vs

    for qk, v in dense_kv:
        a, s = kv_specs(qk, v)
        args += a
        specs += s
    dense_lens = tuple(qk.shape[1] for qk, _ in dense_kv)
    n_keys = sum(dense_lens)
    win_len = 0
    if win_kv is not None:
        assert transposed
        win_len = win_kv[0].shape[1]
        assert win_len >= bq + 2 * WINDOW
        n_keys += bq + 2 * WINDOW
        a, s = kv_specs(*win_kv)
        args += a
        specs += s
    if sink is not None:
        args.append(sink.reshape(1, N_HEADS))
        specs.append(pl.BlockSpec(memory_space=pltpu.SMEM))
    if transposed:
        body = functools.partial(_attn_t_kernel, bq=bq, dense_lens=dense_lens, win_len=win_len,
                                 has_sink=sink is not None)
        scratch = pltpu.VMEM((2, n_keys, bq), F32)
    else:
        body = functools.partial(_attn_rows_kernel, bq=bq, dense_lens=dense_lens, has_sink=sink is not None)
        scratch = pltpu.VMEM((ATTN_SCORE_SLOTS, bq, n_keys), F32)
    return pl.pallas_call(
        body,
        grid=(b, lq // bq),
        in_specs=specs,
        out_specs=pl.BlockSpec((None, bq, q_w), lambda bi, i: (bi, i, 0)),
        out_shape=jax.ShapeDtypeStruct((b, lq, q_w), BF16),
        scratch_shapes=[scratch],
        compiler_params=_params("parallel", "parallel"),
        name="attn_t" if transposed else "attn_rows",
    )(*args)


def _pair_bcast(arr, c0, rows):
    lane = lax.broadcasted_iota(jnp.int32, (rows, LANES), 1)
    return jnp.where(lane < SSM_P, arr[:, c0:c0 + 1], arr[:, c0 + 1:c0 + 2])


def _ssd_kernel(*refs, rev, final, nb):
    x_ref, dt_ref, alog_ref, h0_ref = refs[:4]
    k = 4
    if final:
        yo_ref, z_ref, dsk_ref, ng_ref = refs[k:k + 4]
        k += 4
    y_ref, hl_ref, h_ref = refs[k:k + 3]
    if final:
        gated_ref = refs[k + 3]

    c = pl.program_id(1)
    q = SSM_CHUNK
    rows_blk = x_ref.shape[0]

    @pl.when(c == 0)
    def _():
        h_ref[...] = h0_ref[...]

    ri = lax.broadcasted_iota(jnp.int32, (q, q), 0)
    ci = lax.broadcasted_iota(jnp.int32, (q, q), 1)
    causal = (ci >= ri) if rev else (ci <= ri)
    tri = causal.astype(BF16)
    tri_t = ((ri >= ci) if rev else (ri <= ci)).astype(BF16)
    d_off = SSM_HEADS if rev else 0
    hp = SSM_HEADS // 2
    ppg = hp // SSM_GROUPS
    lane = lax.broadcasted_iota(jnp.int32, (q, LANES), 1)
    last = 0 if rev else q - 1

    n_sub = rows_blk // q
    for sub in (range(n_sub - 1, -1, -1) if rev else range(n_sub)):
        rs = slice(sub * q, (sub + 1) * q)
        xbc = x_ref[rs, :]
        dtv = dt_ref[rs, :]
        dta = dtv * (-jnp.exp(alog_ref[...]))
        dta_t = dta.T
        acs = jnp.zeros((q, LANES), F32)
        acs_t = jnp.zeros((LANES, q), F32)
        for part, part_t in zip(_split3(dta), _split3(dta_t)):
            acs = acs + _dot(tri, part)
            acs_t = acs_t + _dot(part_t, tri_t)
        tot = acs[last:last + 1, :]

        for grp in range(SSM_GROUPS):
            b_g = xbc[:, SSM_INNER + grp * SSM_N:SSM_INNER + (grp + 1) * SSM_N]
            c_g = xbc[:, SSM_INNER + (SSM_GROUPS + grp) * SSM_N:SSM_INNER + (SSM_GROUPS + grp + 1) * SSM_N]
            b_bf = b_g.astype(BF16)
            c_bf = c_g.astype(BF16)
            bt_bf = b_g.T.astype(BF16)
            cb = _dot_nt(c_bf, b_bf)
            for pp in range(ppg):
                pr = grp * ppg + pp
                c0 = d_off + 2 * pr
                sl = slice(pr * LANES, (pr + 1) * LANES)
                xs_p = xbc[:, sl]
                xdt = xs_p * _pair_bcast(dtv, c0, q)
                xdt_bf = xdt.astype(BF16)
                acs_p = _pair_bcast(acs, c0, q)
                tot_p = _pair_bcast(tot, c0, 1)
                y_heads = []
                for hh in range(2):
                    col = c0 + hh
                    seg = acs[:, col:col + 1] - acs_t[col:col + 1, :]
                    lmat = jnp.exp(jnp.where(causal, seg, NEG_INF))
                    y_heads.append(_dot((cb * lmat).astype(BF16), xdt_bf))
                y_diag = jnp.where(lane < SSM_P, y_heads[0], y_heads[1])
                hs = slice(pp * LANES, (pp + 1) * LANES)
                h_p = h_ref[grp, :, hs]
                y_p = y_diag + _dot(c_bf, h_p.astype(BF16)) * jnp.exp(acs_p)
                upd = _dot(bt_bf, (xdt * jnp.exp(tot_p - acs_p)).astype(BF16))
                h_ref[grp, :, hs] = h_p * jnp.exp(tot_p) + upd
                if final:
                    y_all = y_p + yo_ref[rs, sl] + xs_p * dsk_ref[:, sl]
                    gated_ref[rs, sl] = y_all * z_ref[rs, sl]
                else:
                    y_ref[rs, sl] = y_p
    if final:
        gt = gated_ref[...]
        ms = jnp.mean(gt * gt, axis=-1, keepdims=True)
        y_ref[...] = (gt * lax.rsqrt(ms + EPS) * ng_ref[...]).astype(y_ref.dtype)

    @pl.when(c == nb - 1)
    def _():
        hl_ref[...] = h_ref[...]


def _ssd_call(xbc, dt, z_act, y_other, h0, alog128, d_skip, norm_g,
              batch, seq_len, rev):
    final = y_other is not None
    rows_blk = min(SSD_CHUNKS_PER_STEP * SSM_CHUNK, seq_len)
    nb = seq_len // rows_blk

    def bidx(c):
        return (nb - 1 - c) if rev else c

    def main(n):
        return pl.BlockSpec((rows_blk, n), lambda b, c: (b * nb + bidx(c), 0))

    state_spec = pl.BlockSpec((None, SSM_GROUPS, SSM_N, SSM_INNER // SSM_GROUPS),
                              lambda b, c: (b, 0, 0, 0))
    args = [xbc, dt, alog128, h0]
    specs = [main(SSM_XBC), main(LANES), _full((1, LANES)), state_spec]
    scratch = [pltpu.VMEM((SSM_GROUPS, SSM_N, SSM_INNER // SSM_GROUPS), F32)]
    if final:
        args += [y_other, z_act, d_skip, norm_g.reshape(1, SSM_INNER)]
        specs += [main(SSM_INNER), main(SSM_INNER), _full((1, SSM_INNER)), _full((1, SSM_INNER))]
        scratch.append(pltpu.VMEM((rows_blk, SSM_INNER), F32))
    return pl.pallas_call(
        functools.partial(_ssd_kernel, rev=rev, final=final, nb=nb),
        grid=(batch, nb),
        in_specs=specs,
        out_specs=[main(SSM_INNER), state_spec],
        out_shape=[jax.ShapeDtypeStruct((batch * seq_len, SSM_INNER), BF16 if final else F32),
                   jax.ShapeDtypeStruct(h0.shape, F32)],
        scratch_shapes=scratch,
        compiler_params=_params("parallel", "arbitrary"),
        name="ssd_bwd_final" if final else "ssd_fwd",
    )(*args)


def _merge_kernel(x_ref, mod_ref, g_ref, wg_ref, ya_ref, ys_ref, yc_ref, woa_ref, wob_ref, woc_ref,
                  wout_ref, o_ref):
    x = x_ref[...]
    d = x.shape[-1]
    h = _norm_mod(x, g_ref[...], mod_ref[0:1, :], mod_ref[1:2, :]).astype(BF16)
    m = (_sigmoid(_dot(h, wg_ref[:, 0:d])) * _dot(ya_ref[...], woa_ref[...])
         + _sigmoid(_dot(h, wg_ref[:, d:2 * d])) * _dot(ys_ref[...], wob_ref[...])
         + _sigmoid(_dot(h, wg_ref[:, 2 * d:3 * d])) * _dot(yc_ref[...], woc_ref[...]))
    o_ref[...] = x + mod_ref[2:3, :] * _dot(m.astype(BF16), wout_ref[...])


def _merge_call(x2d, mod, g, wg, ya, ys, yc, woa, wob, woc, wout, bm, seq_len, per_seq):
    t, d = x2d.shape
    row = lambda n: pl.BlockSpec((bm, n), lambda i: (i, 0))
    return pl.pallas_call(
        _merge_kernel,
        grid=(t // bm,),
        in_specs=[row(d), _mod_spec(bm, seq_len, per_seq), _full((1, d)), _full(wg.shape),
                  row(ya.shape[1]), row(ys.shape[1]), row(yc.shape[1]),
                  _full(woa.shape), _full(wob.shape), _full(woc.shape), _full(wout.shape)],
        out_specs=row(d),
        out_shape=jax.ShapeDtypeStruct((t, d), F32),
        compiler_params=_params("parallel"),
        name="merge",
    )(x2d, mod, g.reshape(1, d), wg, ya, ys, yc, woa, wob, woc, wout)


def _ffn_kernel(*refs, bm, seq_len, final_norm, ff_chunk):
    x_ref, xp_ref, xn_ref, mod_ref, g_ref, wup_ref, wgt_ref, cw_ref, cb_ref, wdn_ref = refs[:10]
    k = 10
    if final_norm:
        fg_ref = refs[k]
        k += 1
    o_ref = refs[k]
    i = pl.program_id(0)
    x = x_ref[...]
    g = g_ref[...]
    shift, scale = mod_ref[3:4, :], mod_ref[4:5, :]
    hm = _norm_mod(x, g, shift, scale)
    h = hm.astype(BF16)
    h_ext = jnp.concatenate([_norm_mod(xp_ref[...], g, shift, scale), hm,
                             _norm_mod(xn_ref[...], g, shift, scale)], axis=0).astype(BF16)
    pos = (i * bm + lax.broadcasted_iota(jnp.int32, (bm, 1), 0)) % seq_len
    has_prev = pos > 0
    has_next = pos < seq_len - 1
    n_ext = bm + 2 * SUBLANES
    acc = jnp.zeros(x.shape, F32)
    for c0 in range(0, D_FF, ff_chunk):
        cs = slice(c0, c0 + ff_chunk)
        gt = _dot(h_ext, wgt_ref[:, cs])
        gm1 = pltpu.roll(gt, 1, 0)[SUBLANES:SUBLANES + bm, :]
        gp1 = pltpu.roll(gt, n_ext - 1, 0)[SUBLANES:SUBLANES + bm, :]
        conv = (jnp.where(has_prev, gm1, 0.0) * cw_ref[0:1, cs]
                + gt[SUBLANES:SUBLANES + bm, :] * cw_ref[1:2, cs]
                + jnp.where(has_next, gp1, 0.0) * cw_ref[2:3, cs] + cb_ref[:, cs])
        act = (_silu(conv) * _dot(h, wup_ref[:, cs])).astype(BF16)
        acc = acc + _dot(act, wdn_ref[cs, :])
    out = x + mod_ref[5:6, :] * acc
    if final_norm:
        ms = jnp.mean(out * out, axis=-1, keepdims=True)
        out = out * lax.rsqrt(ms + EPS) * fg_ref[...]
    o_ref[...] = out


def _ffn_call(x2d, mod, g, wup, wgt, conv_w, conv_b, wdn, final_g, bm, seq_len, per_seq):
    t, d = x2d.shape
    hb = bm // SUBLANES
    nrow8 = t // SUBLANES
    args = [x2d, x2d, x2d, mod, g.reshape(1, d), wup, wgt, conv_w, conv_b.reshape(1, D_FF), wdn]
    specs = [pl.BlockSpec((bm, d), lambda i: (i, 0)),
             pl.BlockSpec((SUBLANES, d), lambda i: (jnp.maximum(i * hb - 1, 0), 0)),
             pl.BlockSpec((SUBLANES, d), lambda i: (jnp.minimum((i + 1) * hb, nrow8 - 1), 0)),
             _mod_spec(bm, seq_len, per_seq), _full((1, d)), _full(wup.shape), _full(wgt.shape),
             _full(conv_w.shape), _full((1, D_FF)), _full(wdn.shape)]
    if final_g is not None:
        args.append(final_g.reshape(1, d))
        specs.append(_full((1, d)))
    return pl.pallas_call(
        functools.partial(_ffn_kernel, bm=bm, seq_len=seq_len, final_norm=final_g is not None,
                          ff_chunk=D_FF),
        grid=(t // bm,),
        in_specs=specs,
        out_specs=pl.BlockSpec((bm, d), lambda i: (i, 0)),
        out_shape=jax.ShapeDtypeStruct((t, d), F32),
        compiler_params=_params("parallel"),
        name="ffn",
    )(*args)


def _rope_tables(seq_len):
    rows = seq_len // GRID_W
    t_row = jnp.repeat(jnp.arange(rows), GRID_W).astype(F32)
    t_col = jnp.tile(jnp.arange(GRID_W), rows).astype(F32)
    n = HEAD_DIM // 4
    inv = ROPE_BASE ** (-jnp.arange(n, dtype=F32) / n)
    ang = jnp.concatenate([t_row[:, None] * inv, t_col[:, None] * inv], axis=-1)
    cos = jnp.repeat(jnp.cos(ang), 2, axis=-1)
    sin = jnp.repeat(jnp.sin(ang), 2, axis=-1) * jnp.tile(jnp.array([-1.0, 1.0], F32), HEAD_DIM // 2)
    reps = LANES // HEAD_DIM
    return jnp.tile(cos, (1, reps)), jnp.tile(sin, (1, reps))


def _pad_lanes(v):
    return jnp.pad(v.reshape(1, -1), ((0, 0), (0, LANES - v.size)))


def kernel(x, c, ctx, c_ctx, w_mod, b_mod, norm1, norm2, w_in, a_sink, ssm_conv_w, ssm_conv_b,
           ssm_A_log, ssm_dt_bias, ssm_D, ssm_norm, c_q_norm, c_k_norm, w_oa, w_ob, w_oc, w_out,
           ffn_w_up, ffn_w_gate, ffn_conv_w, ffn_conv_b, ffn_w_down, final_norm):
    batch, seq, d = x.shape
    lc = ctx.shape[1]
    depth = w_mod.shape[0]
    bm_lat = min(512, seq)
    bm_ctx = min(512, batch * lc)
    bq_lat = min(256, seq)
    bq_ctx = min(256, lc)

    rope_tabs = _rope_tables(seq)
    c8 = jnp.zeros((SUBLANES, d), F32).at[:batch].set(c).at[batch].set(c_ctx)

    o_aq = 0
    o_bz = o_aq + QKV_W
    o_bx = o_bz + SSM_INNER
    o_bdt = o_bx + SSM_XBC
    o_cq = o_bdt + 2 * SSM_HEADS
    o_g = o_cq + QKV_W

    xl = x.reshape(batch * seq, d)
    xc = ctx.reshape(batch * lc, d)
    zero_state = jnp.zeros((batch, SSM_GROUPS, SSM_N, SSM_INNER // SSM_GROUPS), F32)

    for l in range(depth):
        last = l == depth - 1
        mod = _mod_call(c8, w_mod[l], b_mod[l])
        mod_lat = mod[:batch].reshape(batch, 6, d)
        mod_ctx = mod[batch:batch + 1].reshape(1, 6, d)

        wl = w_in[l]
        w_a = wl[:, o_aq:o_bz].astype(BF16)
        w_z = wl[:, o_bz:o_bx].astype(BF16)
        w_x = wl[:, o_bx:o_bdt].astype(BF16)
        w_dt = jnp.pad(wl[:, o_bdt:o_cq], ((0, 0), (0, LANES - 2 * SSM_HEADS))).astype(BF16)
        w_c = wl[:, o_cq:o_g].astype(BF16)
        w_g = wl[:, o_g:].astype(BF16)
        qk_gains = (c_q_norm[l], c_k_norm[l])
        dtb = _pad_lanes(ssm_dt_bias[l])
        alog = _pad_lanes(ssm_A_log[l])
        d_skip = jnp.repeat(ssm_D[l], SSM_P).reshape(1, SSM_INNER)

        w_all = (w_a, w_c, w_z, w_x, w_dt, dtb, ssm_conv_w[l], ssm_conv_b[l], qk_gains)
        qk_a_c, v_a_c, qk_c_c, v_c_c, z_c, xbc_c, dt_c = _inproj_call(
            xc, mod_ctx, norm1[l], *w_all, bm_ctx, lc, False)
        qk_a, v_a, qk_c, v_c, z_l, xbc_l, dt_l = _inproj_call(
            xl, mod_lat, norm1[l], *w_all, bm_lat, seq, True, rope_tabs)

        kv_a_c = (qk_a_c.reshape(batch, lc, QK_W), v_a_c)
        kv_c_c = (qk_c_c.reshape(batch, lc, QK_W), v_c_c.reshape(batch, lc, N_KV * LANES))
        kv_a = (qk_a.reshape(batch, seq, QK_W), v_a)
        kv_c = (qk_c.reshape(batch, seq, QK_W), v_c.reshape(batch, seq, N_KV * LANES))

        ya = _attn_call(kv_a[0], [kv_a_c], kv_a, a_sink[l], bq_lat, True).reshape(batch * seq, -1)
        yc = _attn_call(kv_c[0], [kv_c, kv_c_c], None, None, bq_lat, False).reshape(batch * seq, -1)

        ssm_args = (alog, d_skip, ssm_norm[l])
        yf_c, hf_c = _ssd_call(xbc_c, dt_c, None, None, zero_state, *ssm_args, batch, lc, False)
        ys_c, hb_c = _ssd_call(xbc_c, dt_c, z_c, yf_c, zero_state, *ssm_args, batch, lc, True)
        yf, _ = _ssd_call(xbc_l, dt_l, None, None, hf_c, *ssm_args, batch, seq, False)
        ys, _ = _ssd_call(xbc_l, dt_l, z_l, yf, hb_c, *ssm_args, batch, seq, True)

        wo = (w_oa[l].astype(BF16), w_ob[l].astype(BF16), w_oc[l].astype(BF16), w_out[l].astype(BF16))
        wf = (ffn_w_up[l].astype(BF16), ffn_w_gate[l].astype(BF16), ffn_conv_w[l], ffn_conv_b[l],
              ffn_w_down[l].astype(BF16))
        xl = _merge_call(xl, mod_lat, norm1[l], w_g, ya, ys, yc, *wo, bm_lat, seq, True)
        xl = _ffn_call(xl, mod_lat, norm2[l], *wf, final_norm if last else None, bm_lat, seq, True)

        if not last:
            ya_c = _attn_call(kv_a_c[0], [kv_a_c], None, a_sink[l], bq_ctx, True).reshape(batch * lc, -1)
            yc_c = _attn_call(kv_c_c[0], [kv_c_c], None, None, bq_ctx, False).reshape(batch * lc, -1)
            xc = _merge_call(xc, mod_ctx, norm1[l], w_g, ya_c, ys_c, yc_c, *wo, bm_ctx, lc, False)
            xc = _ffn_call(xc, mod_ctx, norm2[l], *wf, None, bm_ctx, lc, False)

    return xl.reshape(batch, seq, d)
```

```python
import functools
import math

import jax
import jax.numpy as jnp
from jax import lax
from jax.experimental import pallas as pl
from jax.experimental.pallas import tpu as pltpu

F32 = jnp.float32
BF16 = jnp.bfloat16

D_MODEL = 1024
GRID_W = 64
HEAD_DIM = 64
ROPE_BASE = 10000.0
EPS = 1e-6
WINDOW = 128
N_HEADS = 8
N_KV = 2
QKV_W = (N_HEADS + 2 * N_KV) * HEAD_DIM
QK_W = N_HEADS * HEAD_DIM + N_KV * 128
LOG2E = math.log2(math.e)
Q_SCALE = HEAD_DIM ** -0.5 * LOG2E
ATTN_KEY_TILE = 256
ATTN_T_SUB = 256
ATTN_SCORE_SLOTS = 3
SSM_HEADS = 16
SSM_P = 64
SSM_INNER = SSM_HEADS * SSM_P
SSM_GROUPS = 2
SSM_N = 128
SSM_CHUNK = 128
SSD_CHUNKS_PER_STEP = 8
SSM_XBC = SSM_INNER + 2 * SSM_GROUPS * SSM_N
D_FF = 2816

LANES = 128
SUBLANES = 8
VMEM_LIMIT_BYTES = 56 * 1024 * 1024

NEG_INF = float("-inf")


def _params(*sem):
    return pltpu.CompilerParams(dimension_semantics=sem, vmem_limit_bytes=VMEM_LIMIT_BYTES)


def _full(shape):
    n = len(shape)
    return pl.BlockSpec(shape, lambda *_: (0,) * n, pipeline_mode=pl.Buffered(1))


def _sigmoid(x):
    return 1.0 / (1.0 + jnp.exp(-x))


def _silu(x):
    return x * _sigmoid(x)


def _norm_mod(x, g, shift, scale):
    ms = jnp.mean(x * x, axis=-1, keepdims=True)
    return (x * lax.rsqrt(ms + EPS) * g) * (1.0 + scale) + shift


def _dot(a, b):
    return jnp.dot(a, b, preferred_element_type=F32)


def _dot_nt(a, b):
    return lax.dot_general(a, b, (((1,), (1,)), ((), ())), preferred_element_type=F32)


def _split3(x):
    hi = x.astype(BF16)
    r1 = x - hi.astype(F32)
    mid = r1.astype(BF16)
    lo = (r1 - mid.astype(F32)).astype(BF16)
    return hi, mid, lo


def _mod_kernel(c_ref, w_ref, b_ref, o_ref):
    s = _silu(c_ref[...])
    o_ref[...] = jnp.dot(s, w_ref[...], preferred_element_type=F32,
                         precision=lax.Precision.HIGHEST) + b_ref[...]


def _mod_call(c8, w, b):
    d, n = w.shape
    bn = 1536
    return pl.pallas_call(
        _mod_kernel,
        grid=(n // bn,),
        in_specs=[_full((SUBLANES, d)),
                  pl.BlockSpec((d, bn), lambda j: (0, j)),
                  pl.BlockSpec((1, bn), lambda j: (0, j))],
        out_specs=pl.BlockSpec((SUBLANES, bn), lambda j: (0, j)),
        out_shape=jax.ShapeDtypeStruct((SUBLANES, n), F32),
        compiler_params=_params("arbitrary"),
        name="mod_proj",
    )(c8, w, b.reshape(1, n))


def _mod_spec(bm, seq_len, per_seq):
    if per_seq:
        return pl.BlockSpec((None, 6, D_MODEL), lambda i: ((i * bm) // seq_len, 0, 0))
    return pl.BlockSpec((None, 6, D_MODEL), lambda i: (0, 0, 0))


def _rope(y, cos, sin_signed):
    lane = lax.broadcasted_iota(jnp.int32, y.shape, 1)
    swapped = jnp.where((lane & 1) == 0, pltpu.roll(y, LANES - 1, 1), pltpu.roll(y, 1, 1))
    return y * cos + swapped * sin_signed


def _qkv_epilogue(u, qk_ref, v_ref, rope_tabs, qk_norm, v_transposed):
    n_q = N_HEADS * HEAD_DIM // LANES
    lane = lax.broadcasted_iota(jnp.int32, (u.shape[0], LANES), 1)
    low = lane < HEAD_DIM
    for t in range(n_q + 2):
        y = u[:, t * LANES:(t + 1) * LANES]
        if t <= n_q:
            if qk_norm is not None:
                qg, kg, pool = qk_norm
                sq = y * y
                hi = sq.astype(BF16)
                lo = (sq - hi.astype(F32)).astype(BF16)
                ss = _dot(hi, pool) + _dot(lo, pool)
                y = y * lax.rsqrt(ss * (1.0 / HEAD_DIM) + EPS) * (qg if t < n_q else kg)
            if rope_tabs is not None:
                y = _rope(y, *rope_tabs)
        if t < n_q:
            qk_ref[:, t * LANES:(t + 1) * LANES] = (y * Q_SCALE).astype(BF16)
        elif t == n_q:
            qk_ref[:, t * LANES:(t + 1) * LANES] = jnp.where(low, y, 0.0).astype(BF16)
            qk_ref[:, (t + 1) * LANES:(t + 2) * LANES] = jnp.where(
                low, pltpu.roll(y, HEAD_DIM, 1), 0.0).astype(BF16)
        else:
            ones_col = jnp.where(lane == HEAD_DIM, 1.0, 0.0)
            v0 = jnp.where(low, y, ones_col)
            v1 = jnp.where(low, pltpu.roll(y, HEAD_DIM, 1), ones_col)
            if v_transposed:
                v_ref[0] = v0.T.astype(BF16)
                v_ref[1] = v1.T.astype(BF16)
            else:
                v_ref[:, 0:LANES] = v0.astype(BF16)
                v_ref[:, LANES:2 * LANES] = v1.astype(BF16)


def _inproj_kernel(*refs, rope, bm, seq_len):
    (x_ref, xp_ref, xn_ref, mod_ref, g_ref, wa_ref, wc_ref, wz_ref, wx_ref, wdt_ref, dtb_ref, cw_ref,
     cb_ref, qg_ref, kg_ref, pool_ref) = refs[:16]
    k = 16
    rope_tabs = None
    if rope:
        rope_tabs = (refs[k][...], refs[k + 1][...])
        k += 2
    qka_ref, va_ref, qkc_ref, vc_ref, z_ref, xbc_ref, dt_ref = refs[k:k + 7]
    i = pl.program_id(0)
    g = g_ref[...]
    shift, scale = mod_ref[0:1, :], mod_ref[1:2, :]
    hm = _norm_mod(x_ref[...], g, shift, scale)
    h = hm.astype(BF16)
    h_ext = jnp.concatenate([_norm_mod(xp_ref[...], g, shift, scale), hm,
                             _norm_mod(xn_ref[...], g, shift, scale)], axis=0).astype(BF16)
    _qkv_epilogue(_dot(h, wa_ref[...]), qka_ref, va_ref, rope_tabs, None, True)
    _qkv_epilogue(_dot(h, wc_ref[...]), qkc_ref, vc_ref, rope_tabs,
                  (qg_ref[...], kg_ref[...], pool_ref[...]), False)
    z_ref[...] = _silu(_dot(h, wz_ref[...]))
    dtr = _dot(h, wdt_ref[...]) + dtb_ref[...]
    dt_ref[...] = jnp.maximum(dtr, 0.0) + jnp.log(1.0 + jnp.exp(-jnp.abs(dtr)))
    xe = _dot(h_ext, wx_ref[...])
    n_ext = bm + 2 * SUBLANES
    pos = (i * bm + lax.broadcasted_iota(jnp.int32, (bm, 1), 0)) % seq_len
    xm1 = jnp.where(pos > 0, pltpu.roll(xe, 1, 0)[SUBLANES:SUBLANES + bm, :], 0.0)
    xp1 = jnp.where(pos < seq_len - 1, pltpu.roll(xe, n_ext - 1, 0)[SUBLANES:SUBLANES + bm, :], 0.0)
    xbc_ref[...] = _silu(xm1 * cw_ref[0:1, :] + xe[SUBLANES:SUBLANES + bm, :] * cw_ref[1:2, :]
                         + xp1 * cw_ref[2:3, :] + cb_ref[...])


def _inproj_call(x2d, mod, g, w_a, w_c, w_z, w_x, w_dt, dt_bias128, conv_w, conv_b, qk_gains, bm, seq_len,
                 per_seq, rope_tabs=None):
    t, d = x2d.shape
    rope = rope_tabs is not None
    qg, kg = qk_gains
    lane_head = jnp.arange(LANES) // HEAD_DIM
    pool = (lane_head[:, None] == lane_head[None, :]).astype(BF16)
    hb = bm // SUBLANES
    nrow8 = t // SUBLANES
    args = [x2d, x2d, x2d, mod, g.reshape(1, d), w_a, w_c, w_z, w_x, w_dt, dt_bias128, conv_w,
            conv_b.reshape(1, SSM_XBC),
            jnp.tile(qg, LANES // HEAD_DIM).reshape(1, LANES),
            jnp.tile(kg, LANES // HEAD_DIM).reshape(1, LANES), pool]
    row = lambda n: pl.BlockSpec((bm, n), lambda i: (i, 0))
    specs = [row(d),
             pl.BlockSpec((SUBLANES, d), lambda i: (jnp.maximum(i * hb - 1, 0), 0)),
             pl.BlockSpec((SUBLANES, d), lambda i: (jnp.minimum((i + 1) * hb, nrow8 - 1), 0)),
             _mod_spec(bm, seq_len, per_seq), _full((1, d)), _full(w_a.shape), _full(w_c.shape),
             _full(w_z.shape), _full(w_x.shape), _full(w_dt.shape), _full((1, LANES)),
             _full(conv_w.shape), _full((1, SSM_XBC)),
             _full((1, LANES)), _full((1, LANES)), _full((LANES, LANES))]
    if rope:
        nblk = seq_len // bm
        args += list(rope_tabs)
        specs += [pl.BlockSpec((bm, LANES), lambda i: (i % nblk, 0))] * 2
    return pl.pallas_call(
        functools.partial(_inproj_kernel, rope=rope, bm=bm, seq_len=seq_len),
        grid=(t // bm,),
        in_specs=specs,
        out_specs=[row(QK_W), pl.BlockSpec((N_KV, LANES, bm), lambda i: (0, 0, i)),
                   row(QK_W), row(N_KV * LANES), row(SSM_INNER), row(SSM_XBC), row(LANES)],
        out_shape=[jax.ShapeDtypeStruct((t, QK_W), BF16), jax.ShapeDtypeStruct((N_KV, LANES, t), BF16),
                   jax.ShapeDtypeStruct((t, QK_W), BF16), jax.ShapeDtypeStruct((t, N_KV * LANES), BF16),
                   jax.ShapeDtypeStruct((t, SSM_INNER), F32), jax.ShapeDtypeStruct((t, SSM_XBC), F32),
                   jax.ShapeDtypeStruct((t, LANES), F32)],
        compiler_params=_params("parallel"),
        name="inproj",
    )(*args)


def _attn_t_kernel(*refs, bq, dense_lens, win_len, has_sink):
    q_ref = refs[0]
    k = 1
    segs = []
    for _ in dense_lens:
        segs.append((refs[k:k + 2], refs[k + 2:k + 4]))
        k += 4
    if win_len:
        win_k, win_v = refs[k:k + 2], refs[k + 2:k + 4]
        k += 4
    if has_sink:
        sink_ref = refs[k]
        k += 1
    o_ref, s_ref = refs[k:k + 2]
    i = pl.program_id(1)
    tk = ATTN_KEY_TILE
    sq = s_ref.shape[2]
    hpk = N_HEADS // N_KV

    dense_tiles = [("dense", j, o) for j, lk in enumerate(dense_lens) for o in range(0, lk, tk)]
    starts = {}
    if win_len:
        w = sq + 2 * WINDOW
        for r0 in range(0, bq, sq):
            starts[r0] = pl.multiple_of(jnp.clip(i * bq + r0 - WINDOW, 0, win_len - w), LANES)
        win_tiles = [("win", 0, o) for o in range(0, w, tk)]
        q_iota = lax.broadcasted_iota(jnp.int32, (tk, sq), 1)
        k_iota = lax.broadcasted_iota(jnp.int32, (tk, sq), 0)
    else:
        win_tiles = []
    tiles = dense_tiles + win_tiles

    def k_tile(g, r0, t):
        kind, j, o = t
        if kind == "dense":
            return segs[j][0][g][o:o + tk, :]
        return win_k[g][pl.ds(pl.multiple_of(starts[r0] + o, LANES), tk), :]

    def v_tile(g, r0, t):
        kind, j, o = t
        if kind == "dense":
            return segs[j][1][g][:, o:o + tk]
        return win_v[g][:, pl.ds(pl.multiple_of(starts[r0] + o, LANES), tk)]

    def head_q(r0, h):
        tile = q_ref[r0:r0 + sq, (h // 2) * LANES:(h // 2 + 1) * LANES]
        return tile if h % 2 == 0 else pltpu.roll(tile.astype(F32), HEAD_DIM, 1).astype(BF16)

    def score_tile(c, r0, h, qh, n, t, macc):
        s = _dot_nt(k_tile(h // hpk, r0, t), qh)
        if t[0] == "win":
            rel = (i * bq + r0 + q_iota) - (starts[r0] + t[2] + k_iota)
            s = jnp.where(jnp.abs(rel) <= WINDOW, s, NEG_INF)
        s_ref[c % 2, n * tk:(n + 1) * tk, :] = s
        mt = s.reshape(tk // SUBLANES, SUBLANES, sq).max(axis=0)
        return mt if macc is None else jnp.maximum(macc, mt)

    def finish_max(h, macc):
        m = macc.max(axis=0, keepdims=True)
        if has_sink:
            m = jnp.maximum(m, sink_ref[0, h] * LOG2E)
        return m

    chains = [(r0, h) for r0 in range(0, bq, sq) for h in range(N_HEADS)]
    row = lax.broadcasted_iota(jnp.int32, (LANES, sq), 0)
    r0, h = chains[0]
    qh = head_q(r0, h)
    macc = None
    for n, t in enumerate(tiles):
        macc = score_tile(0, r0, h, qh, n, t, macc)
    m = finish_max(h, macc)
    o_even = None
    for c, (r0, h) in enumerate(chains):
        nxt = c + 1 < len(chains)
        if nxt:
            rn, hn = chains[c + 1]
            qn = head_q(rn, hn)
        macc = None
        acc = jnp.zeros((LANES, sq), F32)
        for n, t in enumerate(tiles):
            if nxt:
                macc = score_tile(c + 1, rn, hn, qn, n, t, macc)
            p = jnp.exp2(s_ref[c % 2, n * tk:(n + 1) * tk, :] - m).astype(BF16)
            acc = acc + _dot(v_tile(h // hpk, r0, t), p)
        l = acc[HEAD_DIM:HEAD_DIM + 1, :]
        if has_sink:
            l = l + jnp.exp2(sink_ref[0, h] * LOG2E - m)
        o = acc / l
        if h % 2 == 0:
            o_even = o
        else:
            pair = jnp.where(row < HEAD_DIM, o_even, pltpu.roll(o, HEAD_DIM, 0))
            o_ref[r0:r0 + sq, (h // 2) * LANES:(h // 2 + 1) * LANES] = pair.T.astype(o_ref.dtype)
        if nxt:
            m = finish_max(hn, macc)


def _attn_rows_kernel(*refs, bq, dense_lens, has_sink):
    q_ref = refs[0]
    k = 1
    segs = []
    for lk in dense_lens:
        segs.append((refs[k:k + 2], refs[k + 2:k + 4], lk))
        k += 4
    if has_sink:
        sink_ref = refs[k]
        k += 1
    o_ref, s_ref = refs[k:k + 2]
    n_slots = s_ref.shape[0]
    lane = lax.broadcasted_iota(jnp.int32, (bq, LANES), 1)
    hpk = N_HEADS // N_KV
    o_even = None
    for h in range(N_HEADS):
        g = h // hpk
        slot = h % n_slots
        tile = q_ref[:, (h // 2) * LANES:(h // 2 + 1) * LANES]
        qh = tile if h % 2 == 0 else pltpu.roll(tile.astype(F32), HEAD_DIM, 1).astype(BF16)
        off = 0
        for k_refs, _, lk in segs:
            s_ref[slot, :, off:off + lk] = _dot_nt(qh, k_refs[g][...])
            off += lk
        s = s_ref[slot]
        m = s.max(axis=-1, keepdims=True)
        if has_sink:
            sink = sink_ref[0, h] * LOG2E
            m = jnp.maximum(m, sink)
        p = jnp.exp2(s - m).astype(BF16)
        acc = jnp.zeros((bq, LANES), F32)
        off = 0
        for _, v_refs, lk in segs:
            acc = acc + _dot(p[:, off:off + lk], v_refs[g][...])
            off += lk
        l = acc[:, HEAD_DIM:HEAD_DIM + 1]
        if has_sink:
            l = l + jnp.exp2(sink - m)
        o = acc / l
        if h % 2 == 0:
            o_even = o
        else:
            pair = jnp.where(lane < HEAD_DIM, o_even, pltpu.roll(o, HEAD_DIM, 1))
            o_ref[:, (h // 2) * LANES:(h // 2 + 1) * LANES] = pair.astype(o_ref.dtype)


def _attn_call(q_arr, dense_kv, win_kv, sink, bq, transposed):
    b, lq, _ = q_arr.shape
    q_w = N_HEADS * HEAD_DIM
    k_blk = q_w // LANES
    args = [q_arr]
    specs = [pl.BlockSpec((None, bq, q_w), lambda bi, i: (bi, i, 0))]

    def kv_specs(qk, v):
        lk = qk.shape[1]
        ks = [pl.BlockSpec((None, lk, LANES), functools.partial(lambda bi, i, j: (bi, 0, j), j=k_blk + j))
              for j in range(N_KV)]
        if transposed:
            vs = [pl.BlockSpec((None, LANES, lk), functools.partial(lambda bi, i, j: (j, 0, bi), j=j))
                  for j in range(N_KV)]
        else:
            vs = [pl.BlockSpec((None, lk, LANES), functools.partial(lambda bi, i, j: (bi, 0, j), j=j))
                  for j in range(N_KV)]
        return [qk] * N_KV + [v] * N_KV, ks + vs

    for qk, v in dense_kv:
        a, s = kv_specs(qk, v)
        args += a
        specs += s
    dense_lens = tuple(qk.shape[1] for qk, _ in dense_kv)
    n_keys = sum(dense_lens)
    win_len = 0
    if win_kv is not None:
        assert transposed
        win_len = win_kv[0].shape[1]
        assert win_len >= min(bq, ATTN_T_SUB) + 2 * WINDOW
        n_keys += min(bq, ATTN_T_SUB) + 2 * WINDOW
        a, s = kv_specs(*win_kv)
        args += a
        specs += s
    if sink is not None:
        args.append(sink.reshape(1, N_HEADS))
        specs.append(pl.BlockSpec(memory_space=pltpu.SMEM))
    if transposed:
        body = functools.partial(_attn_t_kernel, bq=bq, dense_lens=dense_lens, win_len=win_len,
                                 has_sink=sink is not None)
        scratch = pltpu.VMEM((2, n_keys, min(bq, ATTN_T_SUB)), F32)
    else:
        body = functools.partial(_attn_rows_kernel, bq=bq, dense_lens=dense_lens, has_sink=sink is not None)
        scratch = pltpu.VMEM((ATTN_SCORE_SLOTS, bq, n_keys), F32)
    return pl.pallas_call(
        body,
        grid=(b, lq // bq),
        in_specs=specs,
        out_specs=pl.BlockSpec((None, bq, q_w), lambda bi, i: (bi, i, 0)),
        out_shape=jax.ShapeDtypeStruct((b, lq, q_w), BF16),
        scratch_shapes=[scratch],
        compiler_params=_params("parallel", "parallel"),
        name="attn_t" if transposed else "attn_rows",
    )(*args)


def _pair_bcast(arr, c0, rows):
    lane = lax.broadcasted_iota(jnp.int32, (rows, LANES), 1)
    return jnp.where(lane < SSM_P, arr[:, c0:c0 + 1], arr[:, c0 + 1:c0 + 2])


def _ssd_kernel(*refs, rev, final, nb):
    x_ref, dt_ref, alog_ref, h0_ref = refs[:4]
    k = 4
    if final:
        yo_ref, z_ref, dsk_ref, ng_ref = refs[k:k + 4]
        k += 4
    y_ref, hl_ref, h_ref = refs[k:k + 3]
    if final:
        gated_ref = refs[k + 3]

    c = pl.program_id(1)
    q = SSM_CHUNK
    rows_blk = x_ref.shape[0]

    @pl.when(c == 0)
    def _():
        h_ref[...] = h0_ref[...]

    ri = lax.broadcasted_iota(jnp.int32, (q, q), 0)
    ci = lax.broadcasted_iota(jnp.int32, (q, q), 1)
    causal = (ci >= ri) if rev else (ci <= ri)
    tri = causal.astype(BF16)
    tri_t = ((ri >= ci) if rev else (ri <= ci)).astype(BF16)
    d_off = SSM_HEADS if rev else 0
    hp = SSM_HEADS // 2
    ppg = hp // SSM_GROUPS
    lane = lax.broadcasted_iota(jnp.int32, (q, LANES), 1)
    last = 0 if rev else q - 1

    n_sub = rows_blk // q
    for sub in (range(n_sub - 1, -1, -1) if rev else range(n_sub)):
        rs = slice(sub * q, (sub + 1) * q)
        xbc = x_ref[rs, :]
        dtv = dt_ref[rs, :]
        dta = dtv * (-jnp.exp(alog_ref[...]))
        dta_t = dta.T
        acs = jnp.zeros((q, LANES), F32)
        acs_t = jnp.zeros((LANES, q), F32)
        for part, part_t in zip(_split3(dta), _split3(dta_t)):
            acs = acs + _dot(tri, part)
            acs_t = acs_t + _dot(part_t, tri_t)
        tot = acs[last:last + 1, :]

        for grp in range(SSM_GROUPS):
            b_g = xbc[:, SSM_INNER + grp * SSM_N:SSM_INNER + (grp + 1) * SSM_N]
            c_g = xbc[:, SSM_INNER + (SSM_GROUPS + grp) * SSM_N:SSM_INNER + (SSM_GROUPS + grp + 1) * SSM_N]
            b_bf = b_g.astype(BF16)
            c_bf = c_g.astype(BF16)
            bt_bf = b_g.T.astype(BF16)
            cb = _dot_nt(c_bf, b_bf)
            for pp in range(ppg):
                pr = grp * ppg + pp
                c0 = d_off + 2 * pr
                sl = slice(pr * LANES, (pr + 1) * LANES)
                xs_p = xbc[:, sl]
                xdt = xs_p * _pair_bcast(dtv, c0, q)
                xdt_bf = xdt.astype(BF16)
                acs_p = _pair_bcast(acs, c0, q)
                tot_p = _pair_bcast(tot, c0, 1)
                y_heads = []
                for hh in range(2):
                    col = c0 + hh
                    seg = acs[:, col:col + 1] - acs_t[col:col + 1, :]
                    lmat = jnp.exp(jnp.where(causal, seg, NEG_INF))
                    y_heads.append(_dot((cb * lmat).astype(BF16), xdt_bf))
                y_diag = jnp.where(lane < SSM_P, y_heads[0], y_heads[1])
                hs = slice(pp * LANES, (pp + 1) * LANES)
                h_p = h_ref[grp, :, hs]
                y_p = y_diag + _dot(c_bf, h_p.astype(BF16)) * jnp.exp(acs_p)
                upd = _dot(bt_bf, (xdt * jnp.exp(tot_p - acs_p)).astype(BF16))
                h_ref[grp, :, hs] = h_p * jnp.exp(tot_p) + upd
                if final:
                    y_all = y_p + yo_ref[rs, sl] + xs_p * dsk_ref[:, sl]
                    gated_ref[rs, sl] = y_all * z_ref[rs, sl]
                else:
                    y_ref[rs, sl] = y_p
    if final:
        gt = gated_ref[...]
        ms = jnp.mean(gt * gt, axis=-1, keepdims=True)
        y_ref[...] = (gt * lax.rsqrt(ms + EPS) * ng_ref[...]).astype(y_ref.dtype)

    @pl.when(c == nb - 1)
    def _():
        hl_ref[...] = h_ref[...]


def _ssd_call(xbc, dt, z_act, y_other, h0, alog128, d_skip, norm_g,
              batch, seq_len, rev):
    final = y_other is not None
    rows_blk = min(SSD_CHUNKS_PER_STEP * SSM_CHUNK, seq_len)
    nb = seq_len // rows_blk

    def bidx(c):
        return (nb - 1 - c) if rev else c

    def main(n):
        return pl.BlockSpec((rows_blk, n), lambda b, c: (b * nb + bidx(c), 0))

    state_spec = pl.BlockSpec((None, SSM_GROUPS, SSM_N, SSM_INNER // SSM_GROUPS),
                              lambda b, c: (b, 0, 0, 0))
    args = [xbc, dt, alog128, h0]
    specs = [main(SSM_XBC), main(LANES), _full((1, LANES)), state_spec]
    scratch = [pltpu.VMEM((SSM_GROUPS, SSM_N, SSM_INNER // SSM_GROUPS), F32)]
    if final:
        args += [y_other, z_act, d_skip, norm_g.reshape(1, SSM_INNER)]
        specs += [main(SSM_INNER), main(SSM_INNER), _full((1, SSM_INNER)), _full((1, SSM_INNER))]
        scratch.append(pltpu.VMEM((rows_blk, SSM_INNER), F32))
    return pl.pallas_call(
        functools.partial(_ssd_kernel, rev=rev, final=final, nb=nb),
        grid=(batch, nb),
        in_specs=specs,
        out_specs=[main(SSM_INNER), state_spec],
        out_shape=[jax.ShapeDtypeStruct((batch * seq_len, SSM_INNER), BF16 if final else F32),
                   jax.ShapeDtypeStruct(h0.shape, F32)],
        scratch_shapes=scratch,
        compiler_params=_params("parallel", "arbitrary"),
        name="ssd_bwd_final" if final else "ssd_fwd",
    )(*args)


def _merge_kernel(x_ref, mod_ref, g_ref, wg_ref, ya_ref, ys_ref, yc_ref, woa_ref, wob_ref, woc_ref,
                  wout_ref, o_ref):
    x = x_ref[...]
    d = x.shape[-1]
    h = _norm_mod(x, g_ref[...], mod_ref[0:1, :], mod_ref[1:2, :]).astype(BF16)
    m = (_sigmoid(_dot(h, wg_ref[:, 0:d])) * _dot(ya_ref[...], woa_ref[...])
         + _sigmoid(_dot(h, wg_ref[:, d:2 * d])) * _dot(ys_ref[...], wob_ref[...])
         + _sigmoid(_dot(h, wg_ref[:, 2 * d:3 * d])) * _dot(yc_ref[...], woc_ref[...]))
    o_ref[...] = x + mod_ref[2:3, :] * _dot(m.astype(BF16), wout_ref[...])


def _merge_call(x2d, mod, g, wg, ya, ys, yc, woa, wob, woc, wout, bm, seq_len, per_seq):
    t, d = x2d.shape
    row = lambda n: pl.BlockSpec((bm, n), lambda i: (i, 0))
    return pl.pallas_call(
        _merge_kernel,
        grid=(t // bm,),
        in_specs=[row(d), _mod_spec(bm, seq_len, per_seq), _full((1, d)), _full(wg.shape),
                  row(ya.shape[1]), row(ys.shape[1]), row(yc.shape[1]),
                  _full(woa.shape), _full(wob.shape), _full(woc.shape), _full(wout.shape)],
        out_specs=row(d),
        out_shape=jax.ShapeDtypeStruct((t, d), F32),
        compiler_params=_params("parallel"),
        name="merge",
    )(x2d, mod, g.reshape(1, d), wg, ya, ys, yc, woa, wob, woc, wout)


def _ffn_kernel(*refs, bm, seq_len, final_norm, ff_chunk):
    x_ref, xp_ref, xn_ref, mod_ref, g_ref, wup_ref, wgt_ref, cw_ref, cb_ref, wdn_ref = refs[:10]
    k = 10
    if final_norm:
        fg_ref = refs[k]
        k += 1
    o_ref = refs[k]
    i = pl.program_id(0)
    x = x_ref[...]
    g = g_ref[...]
    shift, scale = mod_ref[3:4, :], mod_ref[4:5, :]
    hm = _norm_mod(x, g, shift, scale)
    h = hm.astype(BF16)
    h_ext = jnp.concatenate([_norm_mod(xp_ref[...], g, shift, scale), hm,
                             _norm_mod(xn_ref[...], g, shift, scale)], axis=0).astype(BF16)
    pos = (i * bm + lax.broadcasted_iota(jnp.int32, (bm, 1), 0)) % seq_len
    has_prev = pos > 0
    has_next = pos < seq_len - 1
    n_ext = bm + 2 * SUBLANES
    acc = jnp.zeros(x.shape, F32)
    for c0 in range(0, D_FF, ff_chunk):
        cs = slice(c0, c0 + ff_chunk)
        gt = _dot(h_ext, wgt_ref[:, cs])
        gm1 = pltpu.roll(gt, 1, 0)[SUBLANES:SUBLANES + bm, :]
        gp1 = pltpu.roll(gt, n_ext - 1, 0)[SUBLANES:SUBLANES + bm, :]
        conv = (jnp.where(has_prev, gm1, 0.0) * cw_ref[0:1, cs]
                + gt[SUBLANES:SUBLANES + bm, :] * cw_ref[1:2, cs]
                + jnp.where(has_next, gp1, 0.0) * cw_ref[2:3, cs] + cb_ref[:, cs])
        act = (_silu(conv) * _dot(h, wup_ref[:, cs])).astype(BF16)
        acc = acc + _dot(act, wdn_ref[cs, :])
    out = x + mod_ref[5:6, :] * acc
    if final_norm:
        ms = jnp.mean(out * out, axis=-1, keepdims=True)
        out = out * lax.rsqrt(ms + EPS) * fg_ref[...]
    o_ref[...] = out


def _ffn_call(x2d, mod, g, wup, wgt, conv_w, conv_b, wdn, final_g, bm, seq_len, per_seq):
    t, d = x2d.shape
    hb = bm // SUBLANES
    nrow8 = t // SUBLANES
    args = [x2d, x2d, x2d, mod, g.reshape(1, d), wup, wgt, conv_w, conv_b.reshape(1, D_FF), wdn]
    specs = [pl.BlockSpec((bm, d), lambda i: (i, 0)),
             pl.BlockSpec((SUBLANES, d), lambda i: (jnp.maximum(i * hb - 1, 0), 0)),
             pl.BlockSpec((SUBLANES, d), lambda i: (jnp.minimum((i + 1) * hb, nrow8 - 1), 0)),
             _mod_spec(bm, seq_len, per_seq), _full((1, d)), _full(wup.shape), _full(wgt.shape),
             _full(conv_w.shape), _full((1, D_FF)), _full(wdn.shape)]
    if final_g is not None:
        args.append(final_g.reshape(1, d))
        specs.append(_full((1, d)))
    return pl.pallas_call(
        functools.partial(_ffn_kernel, bm=bm, seq_len=seq_len, final_norm=final_g is not None,
                          ff_chunk=D_FF),
        grid=(t // bm,),
        in_specs=specs,
        out_specs=pl.BlockSpec((bm, d), lambda i: (i, 0)),
        out_shape=jax.ShapeDtypeStruct((t, d), F32),
        compiler_params=_params("parallel"),
        name="ffn",
    )(*args)


def _rope_tables(seq_len):
    rows = seq_len // GRID_W
    t_row = jnp.repeat(jnp.arange(rows), GRID_W).astype(F32)
    t_col = jnp.tile(jnp.arange(GRID_W), rows).astype(F32)
    n = HEAD_DIM // 4
    inv = ROPE_BASE ** (-jnp.arange(n, dtype=F32) / n)
    ang = jnp.concatenate([t_row[:, None] * inv, t_col[:, None] * inv], axis=-1)
    cos = jnp.repeat(jnp.cos(ang), 2, axis=-1)
    sin = jnp.repeat(jnp.sin(ang), 2, axis=-1) * jnp.tile(jnp.array([-1.0, 1.0], F32), HEAD_DIM // 2)
    reps = LANES // HEAD_DIM
    return jnp.tile(cos, (1, reps)), jnp.tile(sin, (1, reps))


def _pad_lanes(v):
    return jnp.pad(v.reshape(1, -1), ((0, 0), (0, LANES - v.size)))


def kernel(x, c, ctx, c_ctx, w_mod, b_mod, norm1, norm2, w_in, a_sink, ssm_conv_w, ssm_conv_b,
           ssm_A_log, ssm_dt_bias, ssm_D, ssm_norm, c_q_norm, c_k_norm, w_oa, w_ob, w_oc, w_out,
           ffn_w_up, ffn_w_gate, ffn_conv_w, ffn_conv_b, ffn_w_down, final_norm):
    batch, seq, d = x.shape
    lc = ctx.shape[1]
    depth = w_mod.shape[0]
    bm_lat = min(512, seq)
    bm_ctx = min(512, batch * lc)
    bq_lat = min(256, seq)
    bq_win = min(1024, seq)
    bq_ctx = min(256, lc)

    rope_tabs = _rope_tables(seq)
    c8 = jnp.zeros((SUBLANES, d), F32).at[:batch].set(c).at[batch].set(c_ctx)

    o_aq = 0
    o_bz = o_aq + QKV_W
    o_bx = o_bz + SSM_INNER
    o_bdt = o_bx + SSM_XBC
    o_cq = o_bdt + 2 * SSM_HEADS
    o_g = o_cq + QKV_W

    xl = x.reshape(batch * seq, d)
    xc = ctx.reshape(batch * lc, d)
    zero_state = jnp.zeros((batch, SSM_GROUPS, SSM_N, SSM_INNER // SSM_GROUPS), F32)

    for l in range(depth):
        last = l == depth - 1
        mod = _mod_call(c8, w_mod[l], b_mod[l])
        mod_lat = mod[:batch].reshape(batch, 6, d)
        mod_ctx = mod[batch:batch + 1].reshape(1, 6, d)

        wl = w_in[l]
        w_a = wl[:, o_aq:o_bz].astype(BF16)
        w_z = wl[:, o_bz:o_bx].astype(BF16)
        w_x = wl[:, o_bx:o_bdt].astype(BF16)
        w_dt = jnp.pad(wl[:, o_bdt:o_cq], ((0, 0), (0, LANES - 2 * SSM_HEADS))).astype(BF16)
        w_c = wl[:, o_cq:o_g].astype(BF16)
        w_g = wl[:, o_g:].astype(BF16)
        qk_gains = (c_q_norm[l], c_k_norm[l])
        dtb = _pad_lanes(ssm_dt_bias[l])
        alog = _pad_lanes(ssm_A_log[l])
        d_skip = jnp.repeat(ssm_D[l], SSM_P).reshape(1, SSM_INNER)

        w_all = (w_a, w_c, w_z, w_x, w_dt, dtb, ssm_conv_w[l], ssm_conv_b[l], qk_gains)
        qk_a_c, v_a_c, qk_c_c, v_c_c, z_c, xbc_c, dt_c = _inproj_call(
            xc, mod_ctx, norm1[l], *w_all, bm_ctx, lc, False)
        qk_a, v_a, qk_c, v_c, z_l, xbc_l, dt_l = _inproj_call(
            xl, mod_lat, norm1[l], *w_all, bm_lat, seq, True, rope_tabs)

        kv_a_c = (qk_a_c.reshape(batch, lc, QK_W), v_a_c)
        kv_c_c = (qk_c_c.reshape(batch, lc, QK_W), v_c_c.reshape(batch, lc, N_KV * LANES))
        kv_a = (qk_a.reshape(batch, seq, QK_W), v_a)
        kv_c = (qk_c.reshape(batch, seq, QK_W), v_c.reshape(batch, seq, N_KV * LANES))

        ya = _attn_call(kv_a[0], [kv_a_c], kv_a, a_sink[l], bq_win, True).reshape(batch * seq, -1)
        yc = _attn_call(kv_c[0], [kv_c, kv_c_c], None, None, bq_lat, False).reshape(batch * seq, -1)

        ssm_args = (alog, d_skip, ssm_norm[l])
        yf_c, hf_c = _ssd_call(xbc_c, dt_c, None, None, zero_state, *ssm_args, batch, lc, False)
        ys_c, hb_c = _ssd_call(xbc_c, dt_c, z_c, yf_c, zero_state, *ssm_args, batch, lc, True)
        yf, _ = _ssd_call(xbc_l, dt_l, None, None, hf_c, *ssm_args, batch, seq, False)
        ys, _ = _ssd_call(xbc_l, dt_l, z_l, yf, hb_c, *ssm_args, batch, seq, True)

        wo = (w_oa[l].astype(BF16), w_ob[l].astype(BF16), w_oc[l].astype(BF16), w_out[l].astype(BF16))
        wf = (ffn_w_up[l].astype(BF16), ffn_w_gate[l].astype(BF16), ffn_conv_w[l], ffn_conv_b[l],
              ffn_w_down[l].astype(BF16))
        xl = _merge_call(xl, mod_lat, norm1[l], w_g, ya, ys, yc, *wo, bm_lat, seq, True)
        xl = _ffn_call(xl, mod_lat, norm2[l], *wf, final_norm if last else None, bm_lat, seq, True)

        if not last:
            ya_c = _attn_call(kv_a_c[0], [kv_a_c], None, a_sink[l], bq_ctx, True).reshape(batch * lc, -1)
            yc_c = _attn_call(kv_c_c[0], [kv_c_c], None, None, bq_ctx, False).reshape(batch * lc, -1)
            xc = _merge_call(xc, mod_ctx, norm1[l], w_g, ya_c, ys_c, yc_c, *wo, bm_ctx, lc, False)
            xc = _ffn_call(xc, mod_ctx, norm2[l], *wf, None, bm_ctx, lc, False)

    return xl.reshape(batch, seq, d)
```

```python
import functools
import math

import jax
import jax.numpy as jnp
from jax import lax
from jax.experimental import pallas as pl
from jax.experimental.pallas import tpu as pltpu

F32 = jnp.float32
BF16 = jnp.bfloat16

D_MODEL = 1024
GRID_W = 64
HEAD_DIM = 64
ROPE_BASE = 10000.0
EPS = 1e-6
WINDOW = 128
N_HEADS = 8
N_KV = 2
QKV_W = (N_HEADS + 2 * N_KV) * HEAD_DIM
QK_W = N_HEADS * HEAD_DIM + N_KV * 128
LOG2E = math.log2(math.e)
Q_SCALE = HEAD_DIM ** -0.5 * LOG2E
ATTN_KEY_TILE = 256
ATTN_T_SUB = 256
ATTN_ROWS_SUB = 256
ATTN_SCORE_SLOTS = 3
SSM_HEADS = 16
SSM_P = 64
SSM_INNER = SSM_HEADS * SSM_P
SSM_GROUPS = 2
SSM_N = 128
SSM_CHUNK = 128
SSD_CHUNKS_PER_STEP = 8
SSM_XBC = SSM_INNER + 2 * SSM_GROUPS * SSM_N
D_FF = 2816

LANES = 128
SUBLANES = 8
VMEM_LIMIT_BYTES = 56 * 1024 * 1024

NEG_INF = float("-inf")


def _params(*sem):
    return pltpu.CompilerParams(dimension_semantics=sem, vmem_limit_bytes=VMEM_LIMIT_BYTES)


def _full(shape):
    n = len(shape)
    return pl.BlockSpec(shape, lambda *_: (0,) * n, pipeline_mode=pl.Buffered(1))


def _sigmoid(x):
    return 1.0 / (1.0 + jnp.exp(-x))


def _silu(x):
    return x * _sigmoid(x)


def _norm_mod(x, g, shift, scale):
    ms = jnp.mean(x * x, axis=-1, keepdims=True)
    return (x * lax.rsqrt(ms + EPS) * g) * (1.0 + scale) + shift


def _dot(a, b):
    return jnp.dot(a, b, preferred_element_type=F32)


def _dot_nt(a, b):
    return lax.dot_general(a, b, (((1,), (1,)), ((), ())), preferred_element_type=F32)


def _split3(x):
    hi = x.astype(BF16)
    r1 = x - hi.astype(F32)
    mid = r1.astype(BF16)
    lo = (r1 - mid.astype(F32)).astype(BF16)
    return hi, mid, lo


def _mod_kernel(c_ref, w_ref, b_ref, o_ref):
    s = _silu(c_ref[...])
    o_ref[...] = jnp.dot(s, w_ref[...], preferred_element_type=F32,
                         precision=lax.Precision.HIGHEST) + b_ref[...]


def _mod_call(c8, w, b):
    d, n = w.shape
    bn = 1536
    return pl.pallas_call(
        _mod_kernel,
        grid=(n // bn,),
        in_specs=[_full((SUBLANES, d)),
                  pl.BlockSpec((d, bn), lambda j: (0, j)),
                  pl.BlockSpec((1, bn), lambda j: (0, j))],
        out_specs=pl.BlockSpec((SUBLANES, bn), lambda j: (0, j)),
        out_shape=jax.ShapeDtypeStruct((SUBLANES, n), F32),
        compiler_params=_params("arbitrary"),
        name="mod_proj",
    )(c8, w, b.reshape(1, n))


def _mod_spec(bm, seq_len, per_seq):
    if per_seq:
        return pl.BlockSpec((None, 6, D_MODEL), lambda i: ((i * bm) // seq_len, 0, 0))
    return pl.BlockSpec((None, 6, D_MODEL), lambda i: (0, 0, 0))


def _rope(y, cos, sin_signed):
    lane = lax.broadcasted_iota(jnp.int32, y.shape, 1)
    swapped = jnp.where((lane & 1) == 0, pltpu.roll(y, LANES - 1, 1), pltpu.roll(y, 1, 1))
    return y * cos + swapped * sin_signed


def _qkv_epilogue(u, qk_ref, v_ref, rope_tabs, qk_norm, v_transposed):
    n_q = N_HEADS * HEAD_DIM // LANES
    lane = lax.broadcasted_iota(jnp.int32, (u.shape[0], LANES), 1)
    low = lane < HEAD_DIM
    for t in range(n_q + 2):
        y = u[:, t * LANES:(t + 1) * LANES]
        if t <= n_q:
            if qk_norm is not None:
                qg, kg, pool = qk_norm
                sq = y * y
                hi = sq.astype(BF16)
                lo = (sq - hi.astype(F32)).astype(BF16)
                ss = _dot(hi, pool) + _dot(lo, pool)
                y = y * lax.rsqrt(ss * (1.0 / HEAD_DIM) + EPS) * (qg if t < n_q else kg)
            if rope_tabs is not None:
                y = _rope(y, *rope_tabs)
        if t < n_q:
            qk_ref[:, t * LANES:(t + 1) * LANES] = (y * Q_SCALE).astype(BF16)
        elif t == n_q:
            qk_ref[:, t * LANES:(t + 1) * LANES] = jnp.where(low, y, 0.0).astype(BF16)
            qk_ref[:, (t + 1) * LANES:(t + 2) * LANES] = jnp.where(
                low, pltpu.roll(y, HEAD_DIM, 1), 0.0).astype(BF16)
        else:
            ones_col = jnp.where(lane == HEAD_DIM, 1.0, 0.0)
            v0 = jnp.where(low, y, ones_col)
            v1 = jnp.where(low, pltpu.roll(y, HEAD_DIM, 1), ones_col)
            if v_transposed:
                v_ref[0] = v0.T.astype(BF16)
                v_ref[1] = v1.T.astype(BF16)
            else:
                v_ref[:, 0:LANES] = v0.astype(BF16)
                v_ref[:, LANES:2 * LANES] = v1.astype(BF16)


def _inproj_kernel(*refs, rope, bm, seq_len):
    (x_ref, xp_ref, xn_ref, mod_ref, g_ref, wa_ref, wc_ref, wz_ref, wx_ref, wdt_ref, dtb_ref, cw_ref,
     cb_ref, qg_ref, kg_ref, pool_ref) = refs[:16]
    k = 16
    rope_tabs = None
    if rope:
        rope_tabs = (refs[k][...], refs[k + 1][...])
        k += 2
    qka_ref, va_ref, qkc_ref, vc_ref, z_ref, xbc_ref, dt_ref = refs[k:k + 7]
    i = pl.program_id(0)
    g = g_ref[...]
    shift, scale = mod_ref[0:1, :], mod_ref[1:2, :]
    hm = _norm_mod(x_ref[...], g, shift, scale)
    h = hm.astype(BF16)
    h_ext = jnp.concatenate([_norm_mod(xp_ref[...], g, shift, scale), hm,
                             _norm_mod(xn_ref[...], g, shift, scale)], axis=0).astype(BF16)
    _qkv_epilogue(_dot(h, wa_ref[...]), qka_ref, va_ref, rope_tabs, None, True)
    _qkv_epilogue(_dot(h, wc_ref[...]), qkc_ref, vc_ref, rope_tabs,
                  (qg_ref[...], kg_ref[...], pool_ref[...]), False)
    z_ref[...] = _silu(_dot(h, wz_ref[...]))
    dtr = _dot(h, wdt_ref[...]) + dtb_ref[...]
    dt_ref[...] = jnp.maximum(dtr, 0.0) + jnp.log(1.0 + jnp.exp(-jnp.abs(dtr)))
    xe = _dot(h_ext, wx_ref[...])
    n_ext = bm + 2 * SUBLANES
    pos = (i * bm + lax.broadcasted_iota(jnp.int32, (bm, 1), 0)) % seq_len
    xm1 = jnp.where(pos > 0, pltpu.roll(xe, 1, 0)[SUBLANES:SUBLANES + bm, :], 0.0)
    xp1 = jnp.where(pos < seq_len - 1, pltpu.roll(xe, n_ext - 1, 0)[SUBLANES:SUBLANES + bm, :], 0.0)
    xbc_ref[...] = _silu(xm1 * cw_ref[0:1, :] + xe[SUBLANES:SUBLANES + bm, :] * cw_ref[1:2, :]
                         + xp1 * cw_ref[2:3, :] + cb_ref[...])


def _inproj_call(x2d, mod, g, w_a, w_c, w_z, w_x, w_dt, dt_bias128, conv_w, conv_b, qk_gains, bm, seq_len,
                 per_seq, rope_tabs=None):
    t, d = x2d.shape
    rope = rope_tabs is not None
    qg, kg = qk_gains
    lane_head = jnp.arange(LANES) // HEAD_DIM
    pool = (lane_head[:, None] == lane_head[None, :]).astype(BF16)
    hb = bm // SUBLANES
    nrow8 = t // SUBLANES
    args = [x2d, x2d, x2d, mod, g.reshape(1, d), w_a, w_c, w_z, w_x, w_dt, dt_bias128, conv_w,
            conv_b.reshape(1, SSM_XBC),
            jnp.tile(qg, LANES // HEAD_DIM).reshape(1, LANES),
            jnp.tile(kg, LANES // HEAD_DIM).reshape(1, LANES), pool]
    row = lambda n: pl.BlockSpec((bm, n), lambda i: (i, 0))
    specs = [row(d),
             pl.BlockSpec((SUBLANES, d), lambda i: (jnp.maximum(i * hb - 1, 0), 0)),
             pl.BlockSpec((SUBLANES, d), lambda i: (jnp.minimum((i + 1) * hb, nrow8 - 1), 0)),
             _mod_spec(bm, seq_len, per_seq), _full((1, d)), _full(w_a.shape), _full(w_c.shape),
             _full(w_z.shape), _full(w_x.shape), _full(w_dt.shape), _full((1, LANES)),
             _full(conv_w.shape), _full((1, SSM_XBC)),
             _full((1, LANES)), _full((1, LANES)), _full((LANES, LANES))]
    if rope:
        nblk = seq_len // bm
        args += list(rope_tabs)
        specs += [pl.BlockSpec((bm, LANES), lambda i: (i % nblk, 0))] * 2
    return pl.pallas_call(
        functools.partial(_inproj_kernel, rope=rope, bm=bm, seq_len=seq_len),
        grid=(t // bm,),
        in_specs=specs,
        out_specs=[row(QK_W), pl.BlockSpec((N_KV, LANES, bm), lambda i: (0, 0, i)),
                   row(QK_W), row(N_KV * LANES), row(SSM_INNER), row(SSM_XBC), row(LANES)],
        out_shape=[jax.ShapeDtypeStruct((t, QK_W), BF16), jax.ShapeDtypeStruct((N_KV, LANES, t), BF16),
                   jax.ShapeDtypeStruct((t, QK_W), BF16), jax.ShapeDtypeStruct((t, N_KV * LANES), BF16),
                   jax.ShapeDtypeStruct((t, SSM_INNER), F32), jax.ShapeDtypeStruct((t, SSM_XBC), F32),
                   jax.ShapeDtypeStruct((t, LANES), F32)],
        compiler_params=_params("parallel"),
        name="inproj",
    )(*args)


def _attn_t_kernel(*refs, bq, dense_lens, win_len, has_sink):
    q_ref = refs[0]
    k = 1
    segs = []
    for _ in dense_lens:
        segs.append((refs[k:k + 2], refs[k + 2:k + 4]))
        k += 4
    if win_len:
        win_k, win_v = refs[k:k + 2], refs[k + 2:k + 4]
        k += 4
    if has_sink:
        sink_ref = refs[k]
        k += 1
    o_ref, s_ref = refs[k:k + 2]
    i = pl.program_id(1)
    tk = ATTN_KEY_TILE
    sq = s_ref.shape[2]
    hpk = N_HEADS // N_KV

    dense_tiles = [("dense", j, o) for j, lk in enumerate(dense_lens) for o in range(0, lk, tk)]
    starts = {}
    if win_len:
        w = sq + 2 * WINDOW
        for r0 in range(0, bq, sq):
            starts[r0] = pl.multiple_of(jnp.clip(i * bq + r0 - WINDOW, 0, win_len - w), LANES)
        win_tiles = [("win", 0, o) for o in range(0, w, tk)]
        q_iota = lax.broadcasted_iota(jnp.int32, (tk, sq), 1)
        k_iota = lax.broadcasted_iota(jnp.int32, (tk, sq), 0)
    else:
        win_tiles = []
    tiles = dense_tiles + win_tiles

    def k_tile(g, r0, t):
        kind, j, o = t
        if kind == "dense":
            return segs[j][0][g][o:o + tk, :]
        return win_k[g][pl.ds(pl.multiple_of(starts[r0] + o, LANES), tk), :]

    def v_tile(g, r0, t):
        kind, j, o = t
        if kind == "dense":
            return segs[j][1][g][:, o:o + tk]
        return win_v[g][:, pl.ds(pl.multiple_of(starts[r0] + o, LANES), tk)]

    def head_q(r0, h):
        tile = q_ref[r0:r0 + sq, (h // 2) * LANES:(h // 2 + 1) * LANES]
        return tile if h % 2 == 0 else pltpu.roll(tile.astype(F32), HEAD_DIM, 1).astype(BF16)

    def score_tile(c, r0, h, qh, n, t, macc):
        s = _dot_nt(k_tile(h // hpk, r0, t), qh)
        if t[0] == "win":
            rel = (i * bq + r0 + q_iota) - (starts[r0] + t[2] + k_iota)
            s = jnp.where(jnp.abs(rel) <= WINDOW, s, NEG_INF)
        s_ref[c % 2, n * tk:(n + 1) * tk, :] = s
        mt = s.reshape(tk // SUBLANES, SUBLANES, sq).max(axis=0)
        return mt if macc is None else jnp.maximum(macc, mt)

    def finish_max(h, macc):
        m = macc.max(axis=0, keepdims=True)
        if has_sink:
            m = jnp.maximum(m, sink_ref[0, h] * LOG2E)
        return m

    chains = [(r0, h) for r0 in range(0, bq, sq) for h in range(N_HEADS)]
    row = lax.broadcasted_iota(jnp.int32, (LANES, sq), 0)
    r0, h = chains[0]
    qh = head_q(r0, h)
    macc = None
    for n, t in enumerate(tiles):
        macc = score_tile(0, r0, h, qh, n, t, macc)
    m = finish_max(h, macc)
    o_even = None
    for c, (r0, h) in enumerate(chains):
        nxt = c + 1 < len(chains)
        if nxt:
            rn, hn = chains[c + 1]
            qn = head_q(rn, hn)
        macc = None
        acc = jnp.zeros((LANES, sq), F32)
        for n, t in enumerate(tiles):
            if nxt:
                macc = score_tile(c + 1, rn, hn, qn, n, t, macc)
            p = jnp.exp2(s_ref[c % 2, n * tk:(n + 1) * tk, :] - m).astype(BF16)
            acc = acc + _dot(v_tile(h // hpk, r0, t), p)
        l = acc[HEAD_DIM:HEAD_DIM + 1, :]
        if has_sink:
            l = l + jnp.exp2(sink_ref[0, h] * LOG2E - m)
        o = acc / l
        if h % 2 == 0:
            o_even = o
        else:
            pair = jnp.where(row < HEAD_DIM, o_even, pltpu.roll(o, HEAD_DIM, 0))
            o_ref[r0:r0 + sq, (h // 2) * LANES:(h // 2 + 1) * LANES] = pair.T.astype(o_ref.dtype)
        if nxt:
            m = finish_max(hn, macc)


def _attn_rows_kernel(*refs, bq, dense_lens, has_sink):
    q_ref = refs[0]
    k = 1
    segs = []
    for lk in dense_lens:
        segs.append((refs[k:k + 2], refs[k + 2:k + 4], lk))
        k += 4
    if has_sink:
        sink_ref = refs[k]
        k += 1
    o_ref, s_ref = refs[k:k + 2]
    n_slots = s_ref.shape[0]
    sq = s_ref.shape[1]
    lane = lax.broadcasted_iota(jnp.int32, (sq, LANES), 1)
    hpk = N_HEADS // N_KV
    chain = 0
    for r0 in range(0, bq, sq):
        o_even = None
        for h in range(N_HEADS):
            g = h // hpk
            slot = chain % n_slots
            chain += 1
            tile = q_ref[r0:r0 + sq, (h // 2) * LANES:(h // 2 + 1) * LANES]
            qh = tile if h % 2 == 0 else pltpu.roll(tile.astype(F32), HEAD_DIM, 1).astype(BF16)
            off = 0
            for k_refs, _, lk in segs:
                s_ref[slot, :, off:off + lk] = _dot_nt(qh, k_refs[g][...])
                off += lk
            s = s_ref[slot]
            m = s.max(axis=-1, keepdims=True)
            if has_sink:
                sink = sink_ref[0, h] * LOG2E
                m = jnp.maximum(m, sink)
            p = jnp.exp2(s - m).astype(BF16)
            acc = jnp.zeros((sq, LANES), F32)
            off = 0
            for _, v_refs, lk in segs:
                acc = acc + _dot(p[:, off:off + lk], v_refs[g][...])
                off += lk
            l = acc[:, HEAD_DIM:HEAD_DIM + 1]
            if has_sink:
                l = l + jnp.exp2(sink - m)
            o = acc / l
            if h % 2 == 0:
                o_even = o
            else:
                pair = jnp.where(lane < HEAD_DIM, o_even, pltpu.roll(o, HEAD_DIM, 1))
                o_ref[r0:r0 + sq, (h // 2) * LANES:(h // 2 + 1) * LANES] = pair.astype(o_ref.dtype)


def _attn_call(q_arr, dense_kv, win_kv, sink, bq, transposed):
    b, lq, _ = q_arr.shape
    q_w = N_HEADS * HEAD_DIM
    k_blk = q_w // LANES
    args = [q_arr]
    specs = [pl.BlockSpec((None, bq, q_w), lambda bi, i: (bi, i, 0))]

    def kv_specs(qk, v):
        lk = qk.shape[1]
        ks = [pl.BlockSpec((None, lk, LANES), functools.partial(lambda bi, i, j: (bi, 0, j), j=k_blk + j))
              for j in range(N_KV)]
        if transposed:
            vs = [pl.BlockSpec((None, LANES, lk), functools.partial(lambda bi, i, j: (j, 0, bi), j=j))
                  for j in range(N_KV)]
        else:
            vs = [pl.BlockSpec((None, lk, LANES), functools.partial(lambda bi, i, j: (bi, 0, j), j=j))
                  for j in range(N_KV)]
        return [qk] * N_KV + [v] * N_KV, ks + vs

    for qk, v in dense_kv:
        a, s = kv_specs(qk, v)
        args += a
        specs += s
    dense_lens = tuple(qk.shape[1] for qk, _ in dense_kv)
    n_keys = sum(dense_lens)
    win_len = 0
    if win_kv is not None:
        assert transposed
        win_len = win_kv[0].shape[1]
        assert win_len >= min(bq, ATTN_T_SUB) + 2 * WINDOW
        n_keys += min(bq, ATTN_T_SUB) + 2 * WINDOW
        a, s = kv_specs(*win_kv)
        args += a
        specs += s
    if sink is not None:
        args.append(sink.reshape(1, N_HEADS))
        specs.append(pl.BlockSpec(memory_space=pltpu.SMEM))
    if transposed:
        body = functools.partial(_attn_t_kernel, bq=bq, dense_lens=dense_lens, win_len=win_len,
                                 has_sink=sink is not None)
        scratch = pltpu.VMEM((2, n_keys, min(bq, ATTN_T_SUB)), F32)
    else:
        body = functools.partial(_attn_rows_kernel, bq=bq, dense_lens=dense_lens, has_sink=sink is not None)
        scratch = pltpu.VMEM((ATTN_SCORE_SLOTS, min(bq, ATTN_ROWS_SUB), n_keys), F32)
    return pl.pallas_call(
        body,
        grid=(b, lq // bq),
        in_specs=specs,
        out_specs=pl.BlockSpec((None, bq, q_w), lambda bi, i: (bi, i, 0)),
        out_shape=jax.ShapeDtypeStruct((b, lq, q_w), BF16),
        scratch_shapes=[scratch],
        compiler_params=_params("parallel", "parallel"),
        name="attn_t" if transposed else "attn_rows",
    )(*args)


def _pair_bcast(arr, c0, rows):
    lane = lax.broadcasted_iota(jnp.int32, (rows, LANES), 1)
    return jnp.where(lane < SSM_P, arr[:, c0:c0 + 1], arr[:, c0 + 1:c0 + 2])


def _ssd_kernel(*refs, rev, final, nb):
    x_ref, dt_ref, alog_ref, h0_ref = refs[:4]
    k = 4
    if final:
        yo_ref, z_ref, dsk_ref, ng_ref = refs[k:k + 4]
        k += 4
    y_ref, hl_ref, h_ref = refs[k:k + 3]
    if final:
        gated_ref = refs[k + 3]

    c = pl.program_id(1)
    q = SSM_CHUNK
    rows_blk = x_ref.shape[0]

    @pl.when(c == 0)
    def _():
        h_ref[...] = h0_ref[...]

    ri = lax.broadcasted_iota(jnp.int32, (q, q), 0)
    ci = lax.broadcasted_iota(jnp.int32, (q, q), 1)
    causal = (ci >= ri) if rev else (ci <= ri)
    tri = causal.astype(BF16)
    tri_t = ((ri >= ci) if rev else (ri <= ci)).astype(BF16)
    d_off = SSM_HEADS if rev else 0
    hp = SSM_HEADS // 2
    ppg = hp // SSM_GROUPS
    lane = lax.broadcasted_iota(jnp.int32, (q, LANES), 1)
    last = 0 if rev else q - 1

    n_sub = rows_blk // q
    for sub in (range(n_sub - 1, -1, -1) if rev else range(n_sub)):
        rs = slice(sub * q, (sub + 1) * q)
        xbc = x_ref[rs, :]
        dtv = dt_ref[rs, :]
        dta = dtv * (-jnp.exp(alog_ref[...]))
        dta_t = dta.T
        acs = jnp.zeros((q, LANES), F32)
        acs_t = jnp.zeros((LANES, q), F32)
        for part, part_t in zip(_split3(dta), _split3(dta_t)):
            acs = acs + _dot(tri, part)
            acs_t = acs_t + _dot(part_t, tri_t)
        tot = acs[last:last + 1, :]

        for grp in range(SSM_GROUPS):
            b_g = xbc[:, SSM_INNER + grp * SSM_N:SSM_INNER + (grp + 1) * SSM_N]
            c_g = xbc[:, SSM_INNER + (SSM_GROUPS + grp) * SSM_N:SSM_INNER + (SSM_GROUPS + grp + 1) * SSM_N]
            b_bf = b_g.astype(BF16)
            c_bf = c_g.astype(BF16)
            bt_bf = b_g.T.astype(BF16)
            cb = _dot_nt(c_bf, b_bf)
            for pp in range(ppg):
                pr = grp * ppg + pp
                c0 = d_off + 2 * pr
                sl = slice(pr * LANES, (pr + 1) * LANES)
                xs_p = xbc[:, sl]
                xdt = xs_p * _pair_bcast(dtv, c0, q)
                xdt_bf = xdt.astype(BF16)
                acs_p = _pair_bcast(acs, c0, q)
                tot_p = _pair_bcast(tot, c0, 1)
                y_heads = []
                for hh in range(2):
                    col = c0 + hh
                    seg = acs[:, col:col + 1] - acs_t[col:col + 1, :]
                    lmat = jnp.exp(jnp.where(causal, seg, NEG_INF))
                    y_heads.append(_dot((cb * lmat).astype(BF16), xdt_bf))
                y_diag = jnp.where(lane < SSM_P, y_heads[0], y_heads[1])
                hs = slice(pp * LANES, (pp + 1) * LANES)
                h_p = h_ref[grp, :, hs]
                y_p = y_diag + _dot(c_bf, h_p.astype(BF16)) * jnp.exp(acs_p)
                upd = _dot(bt_bf, (xdt * jnp.exp(tot_p - acs_p)).astype(BF16))
                h_ref[grp, :, hs] = h_p * jnp.exp(tot_p) + upd
                if final:
                    y_all = y_p + yo_ref[rs, sl] + xs_p * dsk_ref[:, sl]
                    gated_ref[rs, sl] = y_all * z_ref[rs, sl]
                else:
                    y_ref[rs, sl] = y_p
    if final:
        gt = gated_ref[...]
        ms = jnp.mean(gt * gt, axis=-1, keepdims=True)
        y_ref[...] = (gt * lax.rsqrt(ms + EPS) * ng_ref[...]).astype(y_ref.dtype)

    @pl.when(c == nb - 1)
    def _():
        hl_ref[...] = h_ref[...]


def _ssd_call(xbc, dt, z_act, y_other, h0, alog128, d_skip, norm_g,
              batch, seq_len, rev):
    final = y_other is not None
    rows_blk = min(SSD_CHUNKS_PER_STEP * SSM_CHUNK, seq_len)
    nb = seq_len // rows_blk

    def bidx(c):
        return (nb - 1 - c) if rev else c

    def main(n):
        return pl.BlockSpec((rows_blk, n), lambda b, c: (b * nb + bidx(c), 0))

    state_spec = pl.BlockSpec((None, SSM_GROUPS, SSM_N, SSM_INNER // SSM_GROUPS),
                              lambda b, c: (b, 0, 0, 0))
    args = [xbc, dt, alog128, h0]
    specs = [main(SSM_XBC), main(LANES), _full((1, LANES)), state_spec]
    scratch = [pltpu.VMEM((SSM_GROUPS, SSM_N, SSM_INNER // SSM_GROUPS), F32)]
    if final:
        args += [y_other, z_act, d_skip, norm_g.reshape(1, SSM_INNER)]
        specs += [main(SSM_INNER), main(SSM_INNER), _full((1, SSM_INNER)), _full((1, SSM_INNER))]
        scratch.append(pltpu.VMEM((rows_blk, SSM_INNER), F32))
    return pl.pallas_call(
        functools.partial(_ssd_kernel, rev=rev, final=final, nb=nb),
        grid=(batch, nb),
        in_specs=specs,
        out_specs=[main(SSM_INNER), state_spec],
        out_shape=[jax.ShapeDtypeStruct((batch * seq_len, SSM_INNER), BF16 if final else F32),
                   jax.ShapeDtypeStruct(h0.shape, F32)],
        scratch_shapes=scratch,
        compiler_params=_params("parallel", "arbitrary"),
        name="ssd_bwd_final" if final else "ssd_fwd",
    )(*args)


def _merge_kernel(x_ref, mod_ref, g_ref, wg_ref, ya_ref, ys_ref, yc_ref, woa_ref, wob_ref, woc_ref,
                  wout_ref, o_ref):
    x = x_ref[...]
    d = x.shape[-1]
    h = _norm_mod(x, g_ref[...], mod_ref[0:1, :], mod_ref[1:2, :]).astype(BF16)
    m = (_sigmoid(_dot(h, wg_ref[:, 0:d])) * _dot(ya_ref[...], woa_ref[...])
         + _sigmoid(_dot(h, wg_ref[:, d:2 * d])) * _dot(ys_ref[...], wob_ref[...])
         + _sigmoid(_dot(h, wg_ref[:, 2 * d:3 * d])) * _dot(yc_ref[...], woc_ref[...]))
    o_ref[...] = x + mod_ref[2:3, :] * _dot(m.astype(BF16), wout_ref[...])


def _merge_call(x2d, mod, g, wg, ya, ys, yc, woa, wob, woc, wout, bm, seq_len, per_seq):
    t, d = x2d.shape
    row = lambda n: pl.BlockSpec((bm, n), lambda i: (i, 0))
    return pl.pallas_call(
        _merge_kernel,
        grid=(t // bm,),
        in_specs=[row(d), _mod_spec(bm, seq_len, per_seq), _full((1, d)), _full(wg.shape),
                  row(ya.shape[1]), row(ys.shape[1]), row(yc.shape[1]),
                  _full(woa.shape), _full(wob.shape), _full(woc.shape), _full(wout.shape)],
        out_specs=row(d),
        out_shape=jax.ShapeDtypeStruct((t, d), F32),
        compiler_params=_params("parallel"),
        name="merge",
    )(x2d, mod, g.reshape(1, d), wg, ya, ys, yc, woa, wob, woc, wout)


def _ffn_kernel(*refs, bm, seq_len, final_norm, ff_chunk):
    x_ref, xp_ref, xn_ref, mod_ref, g_ref, wup_ref, wgt_ref, cw_ref, cb_ref, wdn_ref = refs[:10]
    k = 10
    if final_norm:
        fg_ref = refs[k]
        k += 1
    o_ref = refs[k]
    i = pl.program_id(0)
    x = x_ref[...]
    g = g_ref[...]
    shift, scale = mod_ref[3:4, :], mod_ref[4:5, :]
    hm = _norm_mod(x, g, shift, scale)
    h = hm.astype(BF16)
    h_ext = jnp.concatenate([_norm_mod(xp_ref[...], g, shift, scale), hm,
                             _norm_mod(xn_ref[...], g, shift, scale)], axis=0).astype(BF16)
    pos = (i * bm + lax.broadcasted_iota(jnp.int32, (bm, 1), 0)) % seq_len
    has_prev = pos > 0
    has_next = pos < seq_len - 1
    n_ext = bm + 2 * SUBLANES
    acc = jnp.zeros(x.shape, F32)
    for c0 in range(0, D_FF, ff_chunk):
        cs = slice(c0, c0 + ff_chunk)
        gt = _dot(h_ext, wgt_ref[:, cs])
        gm1 = pltpu.roll(gt, 1, 0)[SUBLANES:SUBLANES + bm, :]
        gp1 = pltpu.roll(gt, n_ext - 1, 0)[SUBLANES:SUBLANES + bm, :]
        conv = (jnp.where(has_prev, gm1, 0.0) * cw_ref[0:1, cs]
                + gt[SUBLANES:SUBLANES + bm, :] * cw_ref[1:2, cs]
                + jnp.where(has_next, gp1, 0.0) * cw_ref[2:3, cs] + cb_ref[:, cs])
        act = (_silu(conv) * _dot(h, wup_ref[:, cs])).astype(BF16)
        acc = acc + _dot(act, wdn_ref[cs, :])
    out = x + mod_ref[5:6, :] * acc
    if final_norm:
        ms = jnp.mean(out * out, axis=-1, keepdims=True)
        out = out * lax.rsqrt(ms + EPS) * fg_ref[...]
    o_ref[...] = out


def _ffn_call(x2d, mod, g, wup, wgt, conv_w, conv_b, wdn, final_g, bm, seq_len, per_seq):
    t, d = x2d.shape
    hb = bm // SUBLANES
    nrow8 = t // SUBLANES
    args = [x2d, x2d, x2d, mod, g.reshape(1, d), wup, wgt, conv_w, conv_b.reshape(1, D_FF), wdn]
    specs = [pl.BlockSpec((bm, d), lambda i: (i, 0)),
             pl.BlockSpec((SUBLANES, d), lambda i: (jnp.maximum(i * hb - 1, 0), 0)),
             pl.BlockSpec((SUBLANES, d), lambda i: (jnp.minimum((i + 1) * hb, nrow8 - 1), 0)),
             _mod_spec(bm, seq_len, per_seq), _full((1, d)), _full(wup.shape), _full(wgt.shape),
             _full(conv_w.shape), _full((1, D_FF)), _full(wdn.shape)]
    if final_g is not None:
        args.append(final_g.reshape(1, d))
        specs.append(_full((1, d)))
    return pl.pallas_call(
        functools.partial(_ffn_kernel, bm=bm, seq_len=seq_len, final_norm=final_g is not None,
                          ff_chunk=D_FF),
        grid=(t // bm,),
        in_specs=specs,
        out_specs=pl.BlockSpec((bm, d), lambda i: (i, 0)),
        out_shape=jax.ShapeDtypeStruct((t, d), F32),
        compiler_params=_params("parallel"),
        name="ffn",
    )(*args)


def _rope_tables(seq_len):
    rows = seq_len // GRID_W
    t_row = jnp.repeat(jnp.arange(rows), GRID_W).astype(F32)
    t_col = jnp.tile(jnp.arange(GRID_W), rows).astype(F32)
    n = HEAD_DIM // 4
    inv = ROPE_BASE ** (-jnp.arange(n, dtype=F32) / n)
    ang = jnp.concatenate([t_row[:, None] * inv, t_col[:, None] * inv], axis=-1)
    cos = jnp.repeat(jnp.cos(ang), 2, axis=-1)
    sin = jnp.repeat(jnp.sin(ang), 2, axis=-1) * jnp.tile(jnp.array([-1.0, 1.0], F32), HEAD_DIM // 2)
    reps = LANES // HEAD_DIM
    return jnp.tile(cos, (1, reps)), jnp.tile(sin, (1, reps))


def _pad_lanes(v):
    return jnp.pad(v.reshape(1, -1), ((0, 0), (0, LANES - v.size)))


def kernel(x, c, ctx, c_ctx, w_mod, b_mod, norm1, norm2, w_in, a_sink, ssm_conv_w, ssm_conv_b,
           ssm_A_log, ssm_dt_bias, ssm_D, ssm_norm, c_q_norm, c_k_norm, w_oa, w_ob, w_oc, w_out,
           ffn_w_up, ffn_w_gate, ffn_conv_w, ffn_conv_b, ffn_w_down, final_norm):
    batch, seq, d = x.shape
    lc = ctx.shape[1]
    depth = w_mod.shape[0]
    bm_lat = min(512, seq)
    bm_ctx = min(512, batch * lc)
    bq_lat = min(1024, seq)
    bq_win = min(1024, seq)
    bq_ctx = min(256, lc)

    rope_tabs = _rope_tables(seq)
    c8 = jnp.zeros((SUBLANES, d), F32).at[:batch].set(c).at[batch].set(c_ctx)

    o_aq = 0
    o_bz = o_aq + QKV_W
    o_bx = o_bz + SSM_INNER
    o_bdt = o_bx + SSM_XBC
    o_cq = o_bdt + 2 * SSM_HEADS
    o_g = o_cq + QKV_W

    xl = x.reshape(batch * seq, d)
    xc = ctx.reshape(batch * lc, d)
    zero_state = jnp.zeros((batch, SSM_GROUPS, SSM_N, SSM_INNER // SSM_GROUPS), F32)

    for l in range(depth):
        last = l == depth - 1
        mod = _mod_call(c8, w_mod[l], b_mod[l])
        mod_lat = mod[:batch].reshape(batch, 6, d)
        mod_ctx = mod[batch:batch + 1].reshape(1, 6, d)

        wl = w_in[l]
        w_a = wl[:, o_aq:o_bz].astype(BF16)
        w_z = wl[:, o_bz:o_bx].astype(BF16)
        w_x = wl[:, o_bx:o_bdt].astype(BF16)
        w_dt = jnp.pad(wl[:, o_bdt:o_cq], ((0, 0), (0, LANES - 2 * SSM_HEADS))).astype(BF16)
        w_c = wl[:, o_cq:o_g].astype(BF16)
        w_g = wl[:, o_g:].astype(BF16)
        qk_gains = (c_q_norm[l], c_k_norm[l])
        dtb = _pad_lanes(ssm_dt_bias[l])
        alog = _pad_lanes(ssm_A_log[l])
        d_skip = jnp.repeat(ssm_D[l], SSM_P).reshape(1, SSM_INNER)

        w_all = (w_a, w_c, w_z, w_x, w_dt, dtb, ssm_conv_w[l], ssm_conv_b[l], qk_gains)
        qk_a_c, v_a_c, qk_c_c, v_c_c, z_c, xbc_c, dt_c = _inproj_call(
            xc, mod_ctx, norm1[l], *w_all, bm_ctx, lc, False)
        qk_a, v_a, qk_c, v_c, z_l, xbc_l, dt_l = _inproj_call(
            xl, mod_lat, norm1[l], *w_all, bm_lat, seq, True, rope_tabs)

        kv_a_c = (qk_a_c.reshape(batch, lc, QK_W), v_a_c)
        kv_c_c = (qk_c_c.reshape(batch, lc, QK_W), v_c_c.reshape(batch, lc, N_KV * LANES))
        kv_a = (qk_a.reshape(batch, seq, QK_W), v_a)
        kv_c = (qk_c.reshape(batch, seq, QK_W), v_c.reshape(batch, seq, N_KV * LANES))

        ya = _attn_call(kv_a[0], [kv_a_c], kv_a, a_sink[l], bq_win, True).reshape(batch * seq, -1)
        yc = _attn_call(kv_c[0], [kv_c, kv_c_c], None, None, bq_lat, False).reshape(batch * seq, -1)

        ssm_args = (alog, d_skip, ssm_norm[l])
        yf_c, hf_c = _ssd_call(xbc_c, dt_c, None, None, zero_state, *ssm_args, batch, lc, False)
        ys_c, hb_c = _ssd_call(xbc_c, dt_c, z_c, yf_c, zero_state, *ssm_args, batch, lc, True)
        yf, _ = _ssd_call(xbc_l, dt_l, None, None, hf_c, *ssm_args, batch, seq, False)
        ys, _ = _ssd_call(xbc_l, dt_l, z_l, yf, hb_c, *ssm_args, batch, seq, True)

        wo = (w_oa[l].astype(BF16), w_ob[l].astype(BF16), w_oc[l].astype(BF16), w_out[l].astype(BF16))
        wf = (ffn_w_up[l].astype(BF16), ffn_w_gate[l].astype(BF16), ffn_conv_w[l], ffn_conv_b[l],
              ffn_w_down[l].astype(BF16))
        xl = _merge_call(xl, mod_lat, norm1[l], w_g, ya, ys, yc, *wo, bm_lat, seq, True)
        xl = _ffn_call(xl, mod_lat, norm2[l], *wf, final_norm if last else None, bm_lat, seq, True)

        if not last:
            ya_c = _attn_call(kv_a_c[0], [kv_a_c], None, a_sink[l], bq_ctx, True).reshape(batch * lc, -1)
            yc_c = _attn_call(kv_c_c[0], [kv_c_c], None, None, bq_ctx, False).reshape(batch * lc, -1)
            xc = _merge_call(xc, mod_ctx, norm1[l], w_g, ya_c, ys_c, yc_c, *wo, bm_ctx, lc, False)
            xc = _ffn_call(xc, mod_ctx, norm2[l], *wf, None, bm_ctx, lc, False)

    return xl.reshape(batch, seq, d)
```
